```python
import math
import jax
import jax.numpy as jnp
from jax import lax
import numpy as np

D_MODEL = 1024
BATCH = 16
SEQ = 2048
DEPTH = 4

N_MIXERS = 4
N_SSD_LAYERS = (DEPTH + N_MIXERS - 1) // N_MIXERS
N_GLA_LAYERS = (DEPTH + N_MIXERS - 2) // N_MIXERS
N_RWKV_LAYERS = (DEPTH + N_MIXERS - 3) // N_MIXERS
N_S5_LAYERS = (DEPTH + N_MIXERS - 4) // N_MIXERS

DEEPNORM_ALPHA = (2 * DEPTH) ** 0.25
DEEPNORM_BETA = (8 * DEPTH) ** -0.25
LN_EPS = 1e-5
RMS_EPS = 1e-5

SSD_D_INNER = 2 * D_MODEL
SSD_HEAD_DIM = 64
SSD_N_HEADS = SSD_D_INNER // SSD_HEAD_DIM
SSD_N_GROUPS = 4
SSD_D_STATE = 128
SSD_CONV = 4
SSD_CHUNK = 64
SSD_CONV_DIM = SSD_D_INNER + 2 * SSD_N_GROUPS * SSD_D_STATE
SSD_IN_DIM = SSD_D_INNER + SSD_CONV_DIM + SSD_N_HEADS

GLA_N_HEADS = 4
GLA_D_K = D_MODEL // 2
GLA_D_V = D_MODEL
GLA_GATE_RANK = 16
GLA_TAU = 16.0
GLA_CHUNK = 64
GLA_IN_DIM = 2 * GLA_D_K + 2 * GLA_D_V + GLA_GATE_RANK

RWKV_HEAD_DIM = 64
RWKV_N_HEADS = D_MODEL // RWKV_HEAD_DIM
RWKV_DECAY_LORA = 64
RWKV_A_LORA = 64
RWKV_GATE_LORA = 160
RWKV_GN_EPS = 64e-5

S5_GROUP = 16
S5_N_GROUPS = D_MODEL // S5_GROUP
S5_STATE = 64

N_EXPERTS = 16
N_EXPERT_GROUPS = 4
EXPERTS_PER_GROUP = N_EXPERTS // N_EXPERT_GROUPS
TOP_K = 2
D_EXPERT = D_MODEL
MOE_BLOCK = 128

kernel_name = 'hybrid_ssd_gla_rwkv7_s5_moe_deepnorm'


def layer_norm(x, g, b):
    xf = x.astype(jnp.float32)
    mu = jnp.mean(xf, -1, keepdims=True)
    var = jnp.mean(jnp.square(xf - mu), -1, keepdims=True)
    return ((xf - mu) * lax.rsqrt(var + LN_EPS) * g + b).astype(x.dtype)


def rms_norm_f32(y, eps):
    return y * lax.rsqrt(jnp.mean(jnp.square(y), -1, keepdims=True) + eps)


def token_shift(h):
    return jnp.pad(h, ((0, 0), (1, 0), (0, 0)))[:, :-1]


def causal_depthwise_conv(x, w, b):
    k = w.shape[0]
    y = lax.conv_general_dilated(x, w[:, None, :], window_strides=(1,), padding=[(k - 1, 0)],
                                 dimension_numbers=('NWC', 'WIO', 'NWC'),
                                 feature_group_count=x.shape[-1])
    return y + b


def to_chunks(t, chunk):
    b, s = t.shape[:2]
    return jnp.moveaxis(t.reshape((b, s // chunk, chunk) + t.shape[2:]), 1, 0)


def from_chunks(t):
    n, b, l = t.shape[:3]
    return jnp.moveaxis(t, 0, 1).reshape((b, n * l) + t.shape[3:])


def ssd_chunked_scan(x, dt, a, bm, cm):
    bsz, _, g, j, p = x.shape
    n = bm.shape[-1]
    causal = jnp.tril(jnp.ones((SSD_CHUNK, SSD_CHUNK), bool))[None, :, :, None, None]

    def step(state, inp):
        xk, dtk, bk, ck = inp
        acum = jnp.cumsum(dtk * a, axis=1)
        seg = acum[:, :, None] - acum[:, None, :]
        decay = jnp.exp(jnp.where(causal, seg, -jnp.inf))
        cb = jnp.einsum('btgn,bsgn->btsg', ck, bk)
        y_diag = jnp.einsum('btsg,btsgj,bsgj,bsgjp->btgjp', cb, decay, dtk, xk)
        y_off = jnp.einsum('btgn,bgjpn,btgj->btgjp', ck, state, jnp.exp(acum))
        last = acum[:, -1]
        w_s = jnp.exp(last[:, None] - acum) * dtk
        state = state * jnp.exp(last)[..., None, None] + jnp.einsum('bsgn,bsgj,bsgjp->bgjpn', bk, w_s, xk)
        return state, y_diag + y_off

    state0 = jnp.zeros((bsz, g, j, p, n), jnp.float32)
    xs = tuple(to_chunks(t, SSD_CHUNK) for t in (x, dt, bm, cm))
    _, y = lax.scan(step, state0, xs)
    return from_chunks(y)


def ssd_mixer(h, w_in, conv_w, conv_b, dt_bias, a_log, d_skip, norm_g, w_out):
    bsz, seq, _ = h.shape
    f32 = jnp.float32
    hpg = SSD_N_HEADS // SSD_N_GROUPS
    z, xbc, dt = jnp.split(h @ w_in, [SSD_D_INNER, SSD_D_INNER + SSD_CONV_DIM], axis=-1)
    xbc = jax.nn.silu(causal_depthwise_conv(xbc, conv_w, conv_b)).astype(f32)
    xs, bm, cm = jnp.split(xbc, [SSD_D_INNER, SSD_D_INNER + SSD_N_GROUPS * SSD_D_STATE], axis=-1)
    xs = xs.reshape(bsz, seq, SSD_N_GROUPS, hpg, SSD_HEAD_DIM)
    bm = bm.reshape(bsz, seq, SSD_N_GROUPS, SSD_D_STATE)
    cm = cm.reshape(bsz, seq, SSD_N_GROUPS, SSD_D_STATE)
    dt = jax.nn.softplus(dt.astype(f32) + dt_bias).reshape(bsz, seq, SSD_N_GROUPS, hpg)
    a = -jnp.exp(a_log.astype(f32)).reshape(SSD_N_GROUPS, hpg)
    y = ssd_chunked_scan(xs, dt, a, bm, cm) + d_skip.reshape(SSD_N_GROUPS, hpg, 1) * xs
    y = y.reshape(bsz, seq, SSD_D_INNER) * jax.nn.silu(z.astype(f32))
    y = rms_norm_f32(y.reshape(bsz, seq, SSD_N_GROUPS, SSD_D_INNER // SSD_N_GROUPS), RMS_EPS)
    y = y.reshape(bsz, seq, SSD_D_INNER) * norm_g
    return y.astype(h.dtype) @ w_out


def gla_chunked_scan(q, k, v, log_a):
    bsz, _, nh, dk = q.shape
    dv = v.shape[-1]
    causal = jnp.tril(jnp.ones((GLA_CHUNK, GLA_CHUNK), bool))[None, :, :, None, None]

    def step(state, inp):
        qk, kk, vk, gk = inp
        bcum = jnp.cumsum(gk, axis=1)
        rel = jnp.exp(jnp.where(causal, bcum[:, :, None] - bcum[:, None, :], -jnp.inf))
        att = jnp.einsum('bthk,bshk,btshk->bhts', qk, kk, rel)
        o_intra = jnp.einsum('bhts,bshv->bthv', att, vk)
        o_inter = jnp.einsum('bthk,bhkv->bthv', qk * jnp.exp(bcum), state)
        last = bcum[:, -1]
        state = state * jnp.exp(last)[..., None] + jnp.einsum('bshk,bshv->bhkv', kk * jnp.exp(last[:, None] - bcum), vk)
        return state, o_intra + o_inter

    state0 = jnp.zeros((bsz, nh, dk, dv), jnp.float32)
    xs = tuple(to_chunks(t, GLA_CHUNK) for t in (q, k, v, log_a))
    _, o = lax.scan(step, state0, xs)
    return from_chunks(o)


def gla_mixer(h, w_in, w_gate, b_gate, norm_g, w_out):
    bsz, seq, _ = h.shape
    f32 = jnp.float32
    dk = GLA_D_K // GLA_N_HEADS
    dv = GLA_D_V // GLA_N_HEADS
    q, k, v, r, g_low = jnp.split(h @ w_in, [GLA_D_K, 2 * GLA_D_K, 2 * GLA_D_K + GLA_D_V, 2 * GLA_D_K + 2 * GLA_D_V], axis=-1)
    log_a = jax.nn.log_sigmoid((g_low @ w_gate).astype(f32) + b_gate) / GLA_TAU
    q = q.astype(f32).reshape(bsz, seq, GLA_N_HEADS, dk) * (dk ** -0.5)
    k = k.astype(f32).reshape(bsz, seq, GLA_N_HEADS, dk)
    v = v.astype(f32).reshape(bsz, seq, GLA_N_HEADS, dv)
    log_a = log_a.reshape(bsz, seq, GLA_N_HEADS, dk)
    o = rms_norm_f32(gla_chunked_scan(q, k, v, log_a), RMS_EPS)
    o = o.reshape(bsz, seq, GLA_D_V) * norm_g * jax.nn.silu(r.astype(f32))
    return o.astype(h.dtype) @ w_out


def rwkv7_scan(r, w, k, v, kk, a):
    bsz, _, nh, n = r.shape

    def step(state, inp):
        rt, wt, kt, vt, kkt, at = inp
        sa = jnp.einsum('bhij,bhj->bhi', state, -kkt)
        state = state * wt[:, :, None, :] + sa[..., None] * (kkt * at)[:, :, None, :] + vt[..., None] * kt[:, :, None, :]
        return state, jnp.einsum('bhij,bhj->bhi', state, rt)

    xs = tuple(jnp.moveaxis(t, 1, 0) for t in (r, w, k, v, kk, a))
    _, y = lax.scan(step, jnp.zeros((bsz, nh, n, n), jnp.float32), xs)
    return jnp.moveaxis(y, 0, 1)


def rwkv7_mixer(h, mu, w_in, w0, w_w1, w_w2, a0, w_a1, w_a2, w_g1, w_g2, k_k, k_a, r_k, gn_g, gn_b, w_out):
    bsz, seq, d = h.shape
    f32 = jnp.float32
    nh, n = RWKV_N_HEADS, RWKV_HEAD_DIM
    xx = token_shift(h) - h
    xr, xw, xk, xv, xa, xg = (h + xx * mu[i] for i in range(6))
    rkv = jnp.einsum('bsmd,mde->bsme', jnp.stack([xr, xk, xv], axis=2), w_in)
    r, k, v = (rkv[:, :, i].astype(f32) for i in range(3))
    w = -jax.nn.softplus(-(w0 + jnp.tanh(xw @ w_w1) @ w_w2).astype(f32)) - 0.5
    decay = jnp.exp(-jnp.exp(w))
    a = jax.nn.sigmoid((a0 + (xa @ w_a1) @ w_a2).astype(f32))
    g = jax.nn.sigmoid(xg @ w_g1) @ w_g2
    kk = (k * k_k).reshape(bsz, seq, nh, n)
    kk = kk / jnp.maximum(jnp.sqrt(jnp.sum(jnp.square(kk), -1, keepdims=True)), 1e-12)
    k = k * (1.0 + (a - 1.0) * k_a)
    r, k, v, a, decay = (t.reshape(bsz, seq, nh, n) for t in (r, k, v, a, decay))
    y = rwkv7_scan(r, decay, k, v, kk, a)
    mu_y = jnp.mean(y, -1, keepdims=True)
    var_y = jnp.mean(jnp.square(y - mu_y), -1, keepdims=True)
    y = ((y - mu_y) * lax.rsqrt(var_y + RWKV_GN_EPS)).reshape(bsz, seq, d) * gn_g + gn_b
    y = y + (jnp.sum(r * k * r_k, -1, keepdims=True) * v).reshape(bsz, seq, d)
    return (y * g).astype(h.dtype) @ w_out


def complex_linear_combine(e1, e2):
    a1r, a1i, b1r, b1i = e1
    a2r, a2i, b2r, b2i = e2
    return (a2r * a1r - a2i * a1i, a2r * a1i + a2i * a1r,
            a2r * b1r - a2i * b1i + b2r, a2r * b1i + a2i * b1r + b2i)


def s5_mixer(h, w_in, a_re, a_im, log_dt, b_re, b_im, c_re, c_im, d_skip, w_glu, w_out):
    bsz, seq, d = h.shape
    f32 = jnp.float32
    u = (h @ w_in).astype(f32).reshape(bsz, seq, S5_N_GROUPS, S5_GROUP)
    a_re, a_im = a_re.astype(f32), a_im.astype(f32)
    dt = jnp.exp(log_dt.astype(f32))[:, None]
    mag = jnp.exp(a_re * dt)
    ab_re, ab_im = mag * jnp.cos(a_im * dt), mag * jnp.sin(a_im * dt)
    den = jnp.square(a_re) + jnp.square(a_im)
    f_re = ((ab_re - 1.0) * a_re + ab_im * a_im) / den
    f_im = (ab_im * a_re - (ab_re - 1.0) * a_im) / den
    b_re, b_im = b_re.astype(f32), b_im.astype(f32)
    bb_re = f_re[..., None] * b_re - f_im[..., None] * b_im
    bb_im = f_re[..., None] * b_im + f_im[..., None] * b_re
    bu_re = jnp.einsum('bsgc,gpc->bsgp', u, bb_re)
    bu_im = jnp.einsum('bsgc,gpc->bsgp', u, bb_im)
    lam_shape = (1, seq, S5_N_GROUPS, S5_STATE)
    _, _, s_re, s_im = lax.associative_scan(
        complex_linear_combine,
        (jnp.broadcast_to(ab_re, lam_shape), jnp.broadcast_to(ab_im, lam_shape), bu_re, bu_im), axis=1)
    y = jnp.einsum('bsgp,gcp->bsgc', s_re, c_re.astype(f32)) - jnp.einsum('bsgp,gcp->bsgc', s_im, c_im.astype(f32))
    y = (y + d_skip.astype(f32).reshape(S5_N_GROUPS, S5_GROUP) * u).reshape(bsz, seq, d)
    ya, yb = jnp.split(jax.nn.gelu(y).astype(h.dtype) @ w_glu, 2, axis=-1)
    return (ya * jax.nn.sigmoid(yb)) @ w_out


def route(h2, router_w, router_b):
    ntok = h2.shape[0]
    probs = jax.nn.softmax((h2 @ router_w).astype(jnp.float32), axis=-1)
    sel = (probs + router_b).reshape(ntok, N_EXPERT_GROUPS, EXPERTS_PER_GROUP)
    vals, idx = lax.top_k(sel, TOP_K)
    grp = jnp.argmax(jnp.sum(vals, -1), axis=-1)
    eid = grp[:, None] * EXPERTS_PER_GROUP + idx[jnp.arange(ntok), grp]
    wts = jnp.take_along_axis(probs, eid, axis=-1)
    return eid, wts / jnp.sum(wts, -1, keepdims=True)


def moe_ffn(h2, eid, wts, w_up, w_down):
    ntok, d = h2.shape
    n_assign = ntok * TOP_K
    n_blocks = n_assign // MOE_BLOCK + N_EXPERTS
    n_pad = n_blocks * MOE_BLOCK
    eflat = eid.reshape(n_assign)
    order = jnp.argsort(eflat)
    e_sorted = eflat[order]
    tok_sorted = order // TOP_K
    w_sorted = wts.reshape(n_assign)[order]
    counts = jnp.bincount(eflat, length=N_EXPERTS)
    padded = (counts + MOE_BLOCK - 1) // MOE_BLOCK * MOE_BLOCK
    pad_end = jnp.cumsum(padded)
    pad_start = pad_end - padded
    start = jnp.cumsum(counts) - counts
    dest = pad_start[e_sorted] + jnp.arange(n_assign) - start[e_sorted]
    buf_tok = jnp.zeros((n_pad,), jnp.int32).at[dest].set(tok_sorted.astype(jnp.int32))
    buf_w = jnp.zeros((n_pad,), jnp.float32).at[dest].set(w_sorted)
    block_expert = jnp.minimum(jnp.searchsorted(pad_end, jnp.arange(n_blocks) * MOE_BLOCK, side='right'), N_EXPERTS - 1)
    x_blocks = h2[buf_tok].reshape(n_blocks, MOE_BLOCK, d)

    def expert_block(args):
        xb, e = args
        gate, up = jnp.split(xb @ w_up[e], 2, axis=-1)
        return (jax.nn.silu(gate) * up) @ w_down[e]

    y_blocks = lax.map(expert_block, (x_blocks, block_expert)).reshape(n_pad, d)
    y_blocks = y_blocks * buf_w[:, None].astype(h2.dtype)
    return jnp.zeros_like(h2).at[buf_tok].add(y_blocks)


def setup_inputs(seed: int = 0) -> dict:
    key = jax.random.key(seed)
    keys = iter(jax.random.split(key, 96))
    f32 = jnp.float32

    def nrm(shape, scale=1.0):
        return scale * jax.random.normal(next(keys), shape, f32)

    def uni(shape, lo, hi):
        return jax.random.uniform(next(keys), shape, f32, lo, hi)

    def gain(shape):
        return 1.0 + nrm(shape, 0.02)

    d, nl = D_MODEL, DEPTH
    n0, n1, n2, n3 = N_SSD_LAYERS, N_GLA_LAYERS, N_RWKV_LAYERS, N_S5_LAYERS
    beta = DEEPNORM_BETA
    ssd_dt = jnp.exp(uni((n0, SSD_N_HEADS), math.log(1e-3), math.log(1e-1)))
    s5_n = jnp.arange(S5_STATE, dtype=f32)
    return {
        'x': nrm((BATCH, SEQ, d)),
        'c': nrm((BATCH, d)),
        'ada_w': nrm((nl, d, 6 * d), 0.1 * d ** -0.5),
        'ada_b': nrm((nl, 6 * d), 0.02),
        'ln1_g': gain((nl, d)),
        'ln1_b': nrm((nl, d), 0.02),
        'ln2_g': gain((nl, d)),
        'ln2_b': nrm((nl, d), 0.02),
        'ssd_w_in': nrm((n0, d, SSD_IN_DIM), d ** -0.5),
        'ssd_conv_w': nrm((n0, SSD_CONV, SSD_CONV_DIM), SSD_CONV ** -0.5),
        'ssd_conv_b': nrm((n0, SSD_CONV_DIM), 0.02),
        'ssd_dt_bias': ssd_dt + jnp.log(-jnp.expm1(-ssd_dt)),
        'ssd_a_log': jnp.log(uni((n0, SSD_N_HEADS), 1.0, 16.0)),
        'ssd_d': 1.0 + nrm((n0, SSD_N_HEADS), 0.1),
        'ssd_norm_g': gain((n0, SSD_D_INNER)),
        'ssd_w_out': nrm((n0, SSD_D_INNER, d), beta * SSD_D_INNER ** -0.5),
        'gla_w_in': nrm((n1, d, GLA_IN_DIM), d ** -0.5),
        'gla_w_gate': nrm((n1, GLA_GATE_RANK, GLA_D_K), GLA_GATE_RANK ** -0.5),
        'gla_b_gate': nrm((n1, GLA_D_K), 0.1),
        'gla_norm_g': gain((n1, GLA_D_V)),
        'gla_w_out': nrm((n1, GLA_D_V, d), beta * GLA_D_V ** -0.5),
        'rwkv_mu': uni((n2, 6, d), 0.0, 1.0),
        'rwkv_w_in': nrm((n2, 3, d, d), d ** -0.5),
        'rwkv_w0': uni((n2, d), -6.5, -1.5),
        'rwkv_w_w1': nrm((n2, d, RWKV_DECAY_LORA), 0.1 * d ** -0.5),
        'rwkv_w_w2': nrm((n2, RWKV_DECAY_LORA, d), 0.1 * RWKV_DECAY_LORA ** -0.5),
        'rwkv_a0': nrm((n2, d), 0.1),
        'rwkv_w_a1': nrm((n2, d, RWKV_A_LORA), 0.1 * d ** -0.5),
        'rwkv_w_a2': nrm((n2, RWKV_A_LORA, d), 0.1 * RWKV_A_LORA ** -0.5),
        'rwkv_w_g1': nrm((n2, d, RWKV_GATE_LORA), d ** -0.5),
        'rwkv_w_g2': nrm((n2, RWKV_GATE_LORA, d), RWKV_GATE_LORA ** -0.5),
        'rwkv_k_k': 0.85 + nrm((n2, d), 0.02),
        'rwkv_k_a': 1.0 + nrm((n2, d), 0.02),
        'rwkv_r_k': nrm((n2, RWKV_N_HEADS, RWKV_HEAD_DIM), 0.1),
        'rwkv_gn_g': gain((n2, d)),
        'rwkv_gn_b': nrm((n2, d), 0.02),
        'rwkv_w_out': nrm((n2, d, d), beta * d ** -0.5),
        's5_w_in': nrm((n3, d, d), d ** -0.5),
        's5_a_re': -0.5 + nrm((n3, S5_N_GROUPS, S5_STATE), 0.01),
        's5_a_im': math.pi * s5_n + nrm((n3, S5_N_GROUPS, S5_STATE), 0.01),
        's5_log_dt': uni((n3, S5_N_GROUPS), math.log(1e-3), math.log(1e-1)),
        's5_b_re': nrm((n3, S5_N_GROUPS, S5_STATE, S5_GROUP), (2 * S5_GROUP) ** -0.5),
        's5_b_im': nrm((n3, S5_N_GROUPS, S5_STATE, S5_GROUP), (2 * S5_GROUP) ** -0.5),
        's5_c_re': nrm((n3, S5_N_GROUPS, S5_GROUP, S5_STATE), S5_STATE ** -0.5),
        's5_c_im': nrm((n3, S5_N_GROUPS, S5_GROUP, S5_STATE), S5_STATE ** -0.5),
        's5_d': nrm((n3, d)),
        's5_w_glu': nrm((n3, d, 2 * d), d ** -0.5),
        's5_w_out': nrm((n3, d, d), beta * d ** -0.5),
        'moe_w_up': nrm((nl, N_EXPERTS, d, 2 * D_EXPERT), d ** -0.5),
        'moe_w_down': nrm((nl, N_EXPERTS, D_EXPERT, d), beta * D_EXPERT ** -0.5),
        'router_w': nrm((d, N_EXPERTS), d ** -0.5),
        'router_b': nrm((N_EXPERTS,), 0.01),
    }


def reference(x, c, ada_w, ada_b, ln1_g, ln1_b, ln2_g, ln2_b,
              ssd_w_in, ssd_conv_w, ssd_conv_b, ssd_dt_bias, ssd_a_log, ssd_d, ssd_norm_g, ssd_w_out,
              gla_w_in, gla_w_gate, gla_b_gate, gla_norm_g, gla_w_out,
              rwkv_mu, rwkv_w_in, rwkv_w0, rwkv_w_w1, rwkv_w_w2, rwkv_a0, rwkv_w_a1, rwkv_w_a2,
              rwkv_w_g1, rwkv_w_g2, rwkv_k_k, rwkv_k_a, rwkv_r_k, rwkv_gn_g, rwkv_gn_b, rwkv_w_out,
              s5_w_in, s5_a_re, s5_a_im, s5_log_dt, s5_b_re, s5_b_im, s5_c_re, s5_c_im, s5_d, s5_w_glu, s5_w_out,
              moe_w_up, moe_w_down, router_w, router_b):
    bsz, seq, d = x.shape
    c_act = jax.nn.silu(c)
    for i in range(DEPTH):
        kind, j = i % N_MIXERS, i // N_MIXERS
        mod = c_act @ ada_w[i] + ada_b[i]
        sh1, sc1, g1, sh2, sc2, g2 = (t[:, None, :] for t in jnp.split(mod, 6, axis=-1))
        hin = x * (1.0 + sc1) + sh1
        if kind == 0:
            y = ssd_mixer(hin, ssd_w_in[j], ssd_conv_w[j], ssd_conv_b[j], ssd_dt_bias[j], ssd_a_log[j],
                          ssd_d[j], ssd_norm_g[j], ssd_w_out[j])
        elif kind == 1:
            y = gla_mixer(hin, gla_w_in[j], gla_w_gate[j], gla_b_gate[j], gla_norm_g[j], gla_w_out[j])
        elif kind == 2:
            y = rwkv7_mixer(hin, rwkv_mu[j], rwkv_w_in[j], rwkv_w0[j], rwkv_w_w1[j], rwkv_w_w2[j], rwkv_a0[j],
                            rwkv_w_a1[j], rwkv_w_a2[j], rwkv_w_g1[j], rwkv_w_g2[j], rwkv_k_k[j], rwkv_k_a[j],
                            rwkv_r_k[j], rwkv_gn_g[j], rwkv_gn_b[j], rwkv_w_out[j])
        else:
            y = s5_mixer(hin, s5_w_in[j], s5_a_re[j], s5_a_im[j], s5_log_dt[j], s5_b_re[j], s5_b_im[j],
                         s5_c_re[j], s5_c_im[j], s5_d[j], s5_w_glu[j], s5_w_out[j])
        x = layer_norm(DEEPNORM_ALPHA * x + (1.0 + g1) * y, ln1_g[i], ln1_b[i])
        hin = (x * (1.0 + sc2) + sh2).reshape(bsz * seq, d)
        eid, wts = route(hin, router_w, router_b)
        y = moe_ffn(hin, eid, wts, moe_w_up[i], moe_w_down[i]).reshape(bsz, seq, d)
        x = layer_norm(DEEPNORM_ALPHA * x + (1.0 + g2) * y, ln2_g[i], ln2_b[i])
    return x
```

```python
import functools
import math

import jax
import jax.numpy as jnp
from jax import lax
from jax.experimental import pallas as pl
from jax.experimental.pallas import tpu as pltpu

F32 = jnp.float32
BF16 = jnp.bfloat16

D_MODEL = 1024
DEPTH = 4
DEEPNORM_ALPHA = (2 * DEPTH) ** 0.25
LN_EPS = 1e-5
RMS_EPS = 1e-5

SSD_D_INNER = 2048
SSD_HEAD_DIM = 64
SSD_N_HEADS = 32
SSD_N_GROUPS = 4
SSD_D_STATE = 128
SSD_CONV = 4
SSD_CONV_DIM = SSD_D_INNER + 2 * SSD_N_GROUPS * SSD_D_STATE
SSD_CHUNK = 128

GLA_N_HEADS = 4
GLA_D_K = 512
GLA_D_V = 1024
GLA_GATE_RANK = 16
GLA_TAU = 16.0
GLA_SUB = 32
GLA_ROWS = 256

RWKV_HEAD_DIM = 64
RWKV_N_HEADS = 16
RWKV_GN_EPS = 64e-5
RWKV_CHUNK = 64
RWKV_GROUP = 4

S5_GROUP = 16
S5_N_GROUPS = 64
S5_STATE = 64
S5_CHUNK = 16

N_EXPERTS = 16
N_EXPERT_GROUPS = 4
EXPERTS_PER_GROUP = 4
TOP_K = 2
MOE_ROWS = 256

ROW_TILE = 256
LANES = 128
VMEM_LIMIT = 56 * 1024 * 1024
NEG_BIG = -1e30


def _cparams(*sem):
    return pltpu.CompilerParams(dimension_semantics=sem, vmem_limit_bytes=VMEM_LIMIT)


def _dot(a, b):
    return jnp.dot(a.astype(BF16), b.astype(BF16), preferred_element_type=F32)


def _dot_nt(a, b):
    return lax.dot_general(a.astype(BF16), b.astype(BF16), (((1,), (1,)), ((), ())), preferred_element_type=F32)


def _dot_tn(a, b):
    return lax.dot_general(a.astype(BF16), b.astype(BF16), (((0,), (0,)), ((), ())), preferred_element_type=F32)


def _split(x):
    hi = x.astype(BF16)
    lo = (x - hi.astype(F32)).astype(BF16)
    return hi, lo


def _split3(x):
    h1 = x.astype(BF16)
    r1 = x - h1.astype(F32)
    h2 = r1.astype(BF16)
    h3 = (r1 - h2.astype(F32)).astype(BF16)
    return h1, h2, h3


def _dot_x2(a, b_exact):
    hi, lo = _split(a)
    return _dot(hi, b_exact) + _dot(lo, b_exact)


def _dot_x3(a, b_exact):
    h1, h2, h3 = _split3(a)
    return _dot(h1, b_exact) + _dot(h2, b_exact) + _dot(h3, b_exact)


def _dot_hl(a_hi, a_lo, b_hi, b_lo):
    return _dot(a_hi, b_hi) + _dot(a_lo, b_hi) + _dot(a_hi, b_lo)


def _sigmoid(x):
    return 1.0 / (1.0 + jnp.exp(-x))


def _silu(x):
    return x * _sigmoid(x)


def _softplus(x):
    return jnp.maximum(x, 0.0) + jnp.log1p(jnp.exp(-jnp.abs(x)))


def _hilo(w):
    hi, lo = _split(w.astype(F32))
    return jnp.stack([hi, lo])


def _pad_to(a, axis, size):
    pad = [(0, 0)] * a.ndim
    pad[axis] = (0, size - a.shape[axis])
    return jnp.pad(a, pad)


def _modulate(x, m, shift_row, scale_row):
    return x * (1.0 + m[scale_row:scale_row + 1]) + m[shift_row:shift_row + 1]


def _layer_norm(v, g, b):
    mu = jnp.mean(v, axis=-1, keepdims=True)
    vc = v - mu
    var = jnp.mean(vc * vc, axis=-1, keepdims=True)
    return vc * lax.rsqrt(var + LN_EPS) * g + b


def _mod_kernel(c_ref, w_ref, b_ref, o_ref):
    o_ref[0] = _dot(_silu(c_ref[...]), w_ref[0]) + b_ref[0]


def _adaln_mod(c, ada_w, ada_b):
    nl, d, n6 = ada_w.shape
    bsz = c.shape[0]
    tn = 512
    return pl.pallas_call(
        _mod_kernel,
        grid=(nl, n6 // tn),
        in_specs=[
            pl.BlockSpec((bsz, d), lambda l, j: (0, 0)),
            pl.BlockSpec((1, d, tn), lambda l, j: (l, 0, j)),
            pl.BlockSpec((1, 1, tn), lambda l, j: (l, 0, j)),
        ],
        out_specs=pl.BlockSpec((1, bsz, tn), lambda l, j: (l, 0, j)),
        out_shape=jax.ShapeDtypeStruct((nl, bsz, n6), F32),
        compiler_params=_cparams("parallel", "parallel"),
        name="adaln_mod",
    )(c, ada_w, ada_b.reshape(nl, 1, n6))


def _row_spec(width, tm=ROW_TILE):
    return pl.BlockSpec((tm, width), lambda i: (i, 0))


def _full_spec(shape):
    nd = len(shape)
    return pl.BlockSpec(shape, lambda *_: (0,) * nd)


def _mod_spec(tiles_per_seq):
    return pl.BlockSpec((1, 6, D_MODEL), lambda i: (i // tiles_per_seq, 0, 0))


def _ssd_in_kernel(x_ref, mod_ref, wz_ref, wx_ref, wdt_ref, z_ref, xbc_ref, dt_ref):
    h = _modulate(x_ref[...], mod_ref[0], 0, 1)
    hb, hl = _split(h)
    z_ref[...] = jnp.dot(hb, wz_ref[...], preferred_element_type=F32).astype(BF16)
    xbc_ref[...] = jnp.dot(hb, wx_ref[...], preferred_element_type=F32)
    dt_ref[...] = _dot_hl(hb, hl, wdt_ref[0], wdt_ref[1])


def _ssd_scan_kernel(z_ref, xbc_ref, dt_ref, cw_ref, cb_ref, dtb_ref, an_ref, dsk_ref, ng_ref,
                     e_ref, tri_ref, triu_ref, o_ref, xpad, state):
    c = pl.program_id(1)
    L = z_ref.shape[0]
    nstate = SSD_D_STATE

    @pl.when(c == 0)
    def _():
        state[...] = jnp.zeros(state.shape, F32)
        xpad[0:8, :] = jnp.zeros((8, SSD_CONV_DIM), F32)

    @pl.when(c != 0)
    def _():
        xpad[0:8, :] = xpad[L:L + 8, :]

    xpad[8:8 + L, :] = xbc_ref[...]

    acc = cb_ref[...] + cw_ref[3:4, :] * xpad[8:8 + L, :]
    for j in range(1, SSD_CONV):
        acc = acc + cw_ref[3 - j:4 - j, :] * xpad[pl.ds(8 - j, L), :]
    act = _silu(acc)
    xs = act[:, :SSD_D_INNER]
    bm = act[:, SSD_D_INNER:SSD_D_INNER + SSD_N_GROUPS * nstate]
    cm = act[:, SSD_D_INNER + SSD_N_GROUPS * nstate:]

    dtv = _softplus(dt_ref[...] + dtb_ref[...])
    dta = dtv * an_ref[...]
    dhi, dlo = _split(dta)
    tri = tri_ref[...]
    triu = triu_ref[...]
    acum = _dot(tri, dhi) + _dot(tri, dlo)
    acum_t = _dot_tn(dhi, triu) + _dot_tn(dlo, triu)

    e = e_ref[...]
    acum_x = _dot_x3(acum, e)
    dt_x = _dot_x3(dtv, e)
    eac_x = jnp.exp(acum_x)
    last_x = acum_x[L - 1:L, :]
    elast_x = jnp.exp(last_x)
    xdt = xs * dt_x
    xw = xs * (jnp.exp(last_x - acum_x) * dt_x)

    row = lax.broadcasted_iota(jnp.int32, (L, L), 0)
    col = lax.broadcasted_iota(jnp.int32, (L, L), 1)
    causal = row >= col
    lane = lax.broadcasted_iota(jnp.int32, (L, LANES), 1)
    left = lane < SSD_HEAD_DIM

    pieces = []
    heads_per_group = SSD_N_HEADS // SSD_N_GROUPS
    for g in range(SSD_N_GROUPS):
        bg = bm[:, g * nstate:(g + 1) * nstate]
        cg = cm[:, g * nstate:(g + 1) * nstate]
        cb = _dot_nt(cg, bg)
        for p in range(heads_per_group // 2):
            h0 = g * heads_per_group + 2 * p
            pair = h0 // 2
            sl = slice(h0 * SSD_HEAD_DIM, h0 * SSD_HEAD_DIM + LANES)
            ms = []
            for hh in (h0, h0 + 1):
                seg = acum[:, hh:hh + 1] - acum_t[hh:hh + 1, :]
                ms.append((cb * jnp.exp(jnp.where(causal, seg, NEG_BIG))).astype(BF16))
            mcat = jnp.concatenate(ms, axis=1)
            xp = xdt[:, sl]
            xbd = jnp.concatenate([jnp.where(left, xp, 0.0), jnp.where(left, 0.0, xp)], axis=0)
            st = state[pair]
            y_pair = _dot(mcat, xbd) + _dot(cg, st) * eac_x[:, sl]
            state[pair] = st * elast_x[:, sl] + _dot_tn(bg, xw[:, sl])
            pieces.append(y_pair)
    y = jnp.concatenate(pieces, axis=1) + dsk_ref[...] * xs
    y = y * _silu(z_ref[...].astype(F32))
    gw = SSD_D_INNER // SSD_N_GROUPS
    outs = []
    for g in range(SSD_N_GROUPS):
        yg = y[:, g * gw:(g + 1) * gw]
        outs.append(yg * lax.rsqrt(jnp.mean(yg * yg, axis=-1, keepdims=True) + RMS_EPS))
    o_ref[...] = (jnp.concatenate(outs, axis=1) * ng_ref[...]).astype(BF16)


def _ssd_mixer(x2, mod_l, bsz, seq, w_in, conv_w, conv_b, dt_bias, a_log, d_skip, norm_g):
    ntok = x2.shape[0]
    tiles = ntok // ROW_TILE
    tps = seq // ROW_TILE
    wz = w_in[:, :SSD_D_INNER].astype(BF16)
    wx = w_in[:, SSD_D_INNER:SSD_D_INNER + SSD_CONV_DIM].astype(BF16)
    wdt = _hilo(_pad_to(w_in[:, SSD_D_INNER + SSD_CONV_DIM:], 1, LANES))
    z, xbc, dt = pl.pallas_call(
        _ssd_in_kernel,
        grid=(tiles,),
        in_specs=[_row_spec(D_MODEL), _mod_spec(tps), _full_spec(wz.shape), _full_spec(wx.shape),
                  _full_spec(wdt.shape)],
        out_specs=[_row_spec(SSD_D_INNER), _row_spec(SSD_CONV_DIM), _row_spec(LANES)],
        out_shape=[jax.ShapeDtypeStruct((ntok, SSD_D_INNER), BF16),
                   jax.ShapeDtypeStruct((ntok, SSD_CONV_DIM), F32),
                   jax.ShapeDtypeStruct((ntok, LANES), F32)],
        compiler_params=_cparams("parallel"),
        name="ssd_in",
    )(x2, mod_l, wz, wx, wdt)

    L = SSD_CHUNK
    nc = seq // L
    a_neg = _pad_to((-jnp.exp(a_log.astype(F32))).reshape(1, SSD_N_HEADS), 1, LANES)
    dtb = _pad_to(dt_bias.astype(F32).reshape(1, SSD_N_HEADS), 1, LANES)
    dsk = jnp.repeat(d_skip.astype(F32), SSD_HEAD_DIM).reshape(1, SSD_D_INNER)
    expand = (jnp.arange(LANES)[:, None] == (jnp.arange(SSD_D_INNER) // SSD_HEAD_DIM)[None, :]).astype(BF16)
    tri = jnp.tril(jnp.ones((L, L), BF16))
    triu = jnp.triu(jnp.ones((L, L), BF16))
    chunk = lambda w: pl.BlockSpec((L, w), lambda b, c: (b * nc + c, 0))
    full2 = lambda shape: pl.BlockSpec(shape, lambda b, c: (0, 0))
    return pl.pallas_call(
        _ssd_scan_kernel,
        grid=(bsz, nc),
        in_specs=[chunk(SSD_D_INNER), chunk(SSD_CONV_DIM), chunk(LANES),
                  full2((SSD_CONV, SSD_CONV_DIM)), full2((1, SSD_CONV_DIM)), full2((1, LANES)), full2((1, LANES)),
                  full2((1, SSD_D_INNER)), full2((1, SSD_D_INNER)), full2((LANES, SSD_D_INNER)),
                  full2((L, L)), full2((L, L))],
        out_specs=chunk(SSD_D_INNER),
        out_shape=jax.ShapeDtypeStruct((ntok, SSD_D_INNER), BF16),
        scratch_shapes=[pltpu.VMEM((L + 8, SSD_CONV_DIM), F32),
                        pltpu.VMEM((SSD_N_HEADS // 2, SSD_D_STATE, LANES), F32)],
        compiler_params=_cparams("parallel", "arbitrary"),
        name="ssd_scan",
    )(z, xbc, dt, conv_w.astype(F32), conv_b.astype(F32).reshape(1, -1), dtb, a_neg, dsk,
      norm_g.astype(F32).reshape(1, -1), expand, tri, triu)


def _gla_in_kernel(x_ref, mod_ref, wq_ref, wk_ref, wv_ref, wr_ref, wg_ref, wgate_ref, bgate_ref,
                   q_ref, k_ref, v_ref, r_ref, la_ref):
    h = _modulate(x_ref[...], mod_ref[0], 0, 1)
    hb, hl = _split(h)
    dk = GLA_D_K // GLA_N_HEADS
    q_ref[...] = jnp.dot(hb, wq_ref[...], preferred_element_type=F32) * (dk ** -0.5)
    k_ref[...] = jnp.dot(hb, wk_ref[...], preferred_element_type=F32)
    v_ref[...] = jnp.dot(hb, wv_ref[...], preferred_element_type=F32)
    r_ref[...] = jnp.dot(hb, wr_ref[...], preferred_element_type=F32).astype(BF16)
    g_low = _dot_hl(hb, hl, wg_ref[0], wg_ref[1])
    gh, gl = _split(g_low)
    gz = _dot_hl(gh, gl, wgate_ref[0], wgate_ref[1]) + bgate_ref[...]
    la_ref[...] = (jnp.minimum(gz, 0.0) - jnp.log1p(jnp.exp(-jnp.abs(gz)))) * (1.0 / GLA_TAU)


def _gla_scan_kernel(q_ref, k_ref, v_ref, r_ref, la_ref, ng_ref, tri_ref, o_ref, state, bc_sc):
    c = pl.program_id(1)
    sub = GLA_SUB
    dk = GLA_D_K // GLA_N_HEADS
    dv = GLA_D_V // GLA_N_HEADS

    @pl.when(c == 0)
    def _():
        state[...] = jnp.zeros(state.shape, F32)

    tri = tri_ref[...]
    rows = lax.broadcasted_iota(jnp.int32, (sub, 1), 0)
    ng = ng_ref[...]

    def sub_chunk(s, carry):
        base = pl.multiple_of(s * sub, sub)
        q = q_ref[pl.ds(base, sub), :]
        k = k_ref[pl.ds(base, sub), :]
        v = v_ref[pl.ds(base, sub), :]
        la = la_ref[pl.ds(base, sub), :]
        lhi, llo = _split(la)
        bcum = _dot(tri, lhi) + _dot(tri, llo)
        bc_sc[...] = bcum

        def key_step(j, o):
            krow = k_ref[pl.ds(base + j, 1), :]
            vrow = v_ref[pl.ds(base + j, 1), :]
            grow = bc_sc[pl.ds(j, 1), :]
            ex = jnp.exp(jnp.where(rows >= j, bcum - grow, NEG_BIG))
            xj = q * krow * ex
            outs = []
            for h in range(GLA_N_HEADS):
                a = jnp.sum(xj[:, h * dk:(h + 1) * dk], axis=-1, keepdims=True)
                outs.append(a * vrow[:, h * dv:(h + 1) * dv])
            return o + jnp.concatenate(outs, axis=1)

        o = lax.fori_loop(0, sub, key_step, jnp.zeros((sub, GLA_D_V), F32))

        last = bcum[sub - 1:sub, :]
        qdec = q * jnp.exp(bcum)
        kdec = k * jnp.exp(last - bcum)
        elast = jnp.exp(last)
        outs = []
        for h in range(GLA_N_HEADS):
            st = state[h]
            oh = o[:, h * dv:(h + 1) * dv] + _dot_nt(qdec[:, h * dk:(h + 1) * dk], st)
            state[h] = st * elast[:, h * dk:(h + 1) * dk] + _dot_tn(v[:, h * dv:(h + 1) * dv],
                                                                     kdec[:, h * dk:(h + 1) * dk])
            outs.append(oh * lax.rsqrt(jnp.mean(oh * oh, axis=-1, keepdims=True) + RMS_EPS))
        on = jnp.concatenate(outs, axis=1) * ng
        o_ref[pl.ds(base, sub), :] = (on * _silu(r_ref[pl.ds(base, sub), :].astype(F32))).astype(BF16)
        return carry

    lax.fori_loop(0, q_ref.shape[0] // sub, sub_chunk, 0)


def _gla_mixer(x2, mod_l, bsz, seq, w_in, w_gate, b_gate, norm_g):
    ntok = x2.shape[0]
    tiles = ntok // ROW_TILE
    tps = seq // ROW_TILE
    o1, o2, o3 = GLA_D_K, 2 * GLA_D_K, 2 * GLA_D_K + GLA_D_V
    o4 = o3 + GLA_D_V
    wq, wk = w_in[:, :o1].astype(BF16), w_in[:, o1:o2].astype(BF16)
    wv, wr = w_in[:, o2:o3].astype(BF16), w_in[:, o3:o4].astype(BF16)
    wg = _hilo(_pad_to(w_in[:, o4:], 1, LANES))
    wgate = _hilo(_pad_to(w_gate, 0, LANES))
    q, k, v, r, la = pl.pallas_call(
        _gla_in_kernel,
        grid=(tiles,),
        in_specs=[_row_spec(D_MODEL), _mod_spec(tps), _full_spec(wq.shape), _full_spec(wk.shape),
                  _full_spec(wv.shape), _full_spec(wr.shape), _full_spec(wg.shape), _full_spec(wgate.shape),
                  _full_spec((1, GLA_D_K))],
        out_specs=[_row_spec(GLA_D_K), _row_spec(GLA_D_K), _row_spec(GLA_D_V), _row_spec(GLA_D_V),
                   _row_spec(GLA_D_K)],
        out_shape=[jax.ShapeDtypeStruct((ntok, GLA_D_K), F32), jax.ShapeDtypeStruct((ntok, GLA_D_K), F32),
                   jax.ShapeDtypeStruct((ntok, GLA_D_V), F32), jax.ShapeDtypeStruct((ntok, GLA_D_V), BF16),
                   jax.ShapeDtypeStruct((ntok, GLA_D_K), F32)],
        compiler_params=_cparams("parallel"),
        name="gla_in",
    )(x2, mod_l, wq, wk, wv, wr, wg, wgate, b_gate.astype(F32).reshape(1, -1))

    rows = GLA_ROWS
    nc = seq // rows
    chunk = lambda w: pl.BlockSpec((rows, w), lambda b, c: (b * nc + c, 0))
    full2 = lambda shape: pl.BlockSpec(shape, lambda b, c: (0, 0))
    tri = jnp.tril(jnp.ones((GLA_SUB, GLA_SUB), BF16))
    return pl.pallas_call(
        _gla_scan_kernel,
        grid=(bsz, nc),
        in_specs=[chunk(GLA_D_K), chunk(GLA_D_K), chunk(GLA_D_V), chunk(GLA_D_V), chunk(GLA_D_K),
                  full2((1, GLA_D_V)), full2((GLA_SUB, GLA_SUB))],
        out_specs=chunk(GLA_D_V),
        out_shape=jax.ShapeDtypeStruct((ntok, GLA_D_V), BF16),
        scratch_shapes=[pltpu.VMEM((GLA_N_HEADS, GLA_D_V // GLA_N_HEADS, GLA_D_K // GLA_N_HEADS), F32),
                        pltpu.VMEM((GLA_SUB, GLA_D_K), F32)],
        compiler_params=_cparams("parallel", "arbitrary"),
        name="gla_scan",
    )(q, k, v, r, la, norm_g.astype(F32).reshape(1, -1), tri)


def _rwkv_in_kernel(x_ref, mod_ref, mu_ref, wr_ref, wk_ref, wv_ref, ww1_ref, ww2_ref, wa1_ref, wa2_ref,
                    wg1_ref, wg2_ref, vec_ref, bd_ref,
                    r_ref, lw_ref, k2_ref, v_ref, kk_ref, b_ref, g_ref, bonus_ref, hp, *, tiles_per_seq):
    i = pl.program_id(0)
    tm = x_ref.shape[0]
    h = _modulate(x_ref[...], mod_ref[0], 0, 1)

    @pl.when(i % tiles_per_seq == 0)
    def _():
        hp[0:8, :] = jnp.zeros((8, D_MODEL), F32)

    @pl.when(i % tiles_per_seq != 0)
    def _():
        hp[0:8, :] = hp[tm:tm + 8, :]

    hp[8:8 + tm, :] = h
    xx = hp[pl.ds(7, tm), :] - h
    mu = mu_ref[...]
    xr, xw, xk, xv, xa, xg = (h + xx * mu[j:j + 1] for j in range(6))
    vec = vec_ref[...]
    w0, a0, k_k, k_a, r_k = (vec[j:j + 1] for j in range(5))
    bd = bd_ref[...]

    r = _dot(xr, wr_ref[...])
    k = _dot(xk, wk_ref[...])
    v = _dot(xv, wv_ref[...])
    wlog = -_softplus(-(w0 + _dot(jnp.tanh(_dot(xw, ww1_ref[...])), ww2_ref[...]))) - 0.5
    a = _sigmoid(a0 + _dot(_dot(xa, wa1_ref[...]), wa2_ref[...]))
    g = _dot(_sigmoid(_dot(xg, wg1_ref[...])), wg2_ref[...])
    kx = k * k_k
    norm = jnp.sqrt(_dot_x2(kx * kx, bd))
    kk = kx / jnp.maximum(norm, 1e-12)
    k2 = k * (1.0 + (a - 1.0) * k_a)
    bonus = _dot_x2(r * k2 * r_k, bd) * v

    r_ref[...] = r.astype(BF16)
    lw_ref[...] = -jnp.exp(wlog)
    k2_ref[...] = k2.astype(BF16)
    v_ref[...] = v.astype(BF16)
    kk_ref[...] = kk.astype(BF16)
    b_ref[...] = (kk * a).astype(BF16)
    g_ref[...] = g.astype(BF16)
    bonus_ref[...] = bonus.astype(BF16)


def _rwkv_scan_kernel(r_ref, lw_ref, k2_ref, v_ref, kk_ref, b_ref, g_ref, bonus_ref, gn_ref, tri_ref, bd_ref,
                      o_ref, state):
    c = pl.program_id(1)
    C = RWKV_CHUNK
    n = RWKV_HEAD_DIM
    gw = RWKV_GROUP * n
    ngroups = D_MODEL // gw

    @pl.when(c == 0)
    def _():
        state[...] = jnp.zeros(state.shape, F32)

    tri = tri_ref[...]
    ri = lax.broadcasted_iota(jnp.int32, (gw, gw), 0)
    ci = lax.broadcasted_iota(jnp.int32, (gw, gw), 1)
    shift = n.bit_length() - 1
    same = (ri >> shift) == (ci >> shift)
    strict = same & ((ri & (C - 1)) > (ci & (C - 1)))
    incl = same & ((ri & (C - 1)) >= (ci & (C - 1)))
    lane_head = lax.broadcasted_iota(jnp.int32, (C, gw), 1) >> shift

    def expand(a):
        return jnp.concatenate([jnp.where(lane_head == hh, a, 0.0) for hh in range(RWKV_GROUP)], axis=0)

    def collapse(a):
        out = a[0:C]
        for hh in range(1, RWKV_GROUP):
            out = out + a[hh * C:(hh + 1) * C]
        return out

    ys = []
    for gi in range(ngroups):
        sl = slice(gi * gw, (gi + 1) * gw)
        lw = lw_ref[:, sl]
        r = r_ref[:, sl].astype(F32)
        k2 = k2_ref[:, sl].astype(F32)
        v = v_ref[:, sl]
        kk = kk_ref[:, sl].astype(F32)
        b = b_ref[:, sl].astype(F32)
        lhi, llo = _split(lw)
        G = _dot(tri, lhi) + _dot(tri, llo)
        gc = G[C - 1:C, :]
        e_g = jnp.exp(G)
        e_ng = jnp.exp(-G)
        e_c = jnp.exp(gc - G)
        rt = r * e_g
        kt = kk * jnp.exp(G - lw)
        kt_e, rt_e = expand(kt), expand(rt)
        kh_e, bh_e, v_e = expand(k2 * e_ng), expand(b * e_ng), expand(v.astype(F32))
        n_mat = jnp.where(strict, _dot_nt(kt_e, bh_e), 0.0)
        m_kk = jnp.where(strict, _dot_nt(kt_e, kh_e), 0.0)
        m_rk = jnp.where(incl, _dot_nt(rt_e, kh_e), 0.0)
        m_rb = jnp.where(incl, _dot_nt(rt_e, bh_e), 0.0)
        st = state[gi]
        u = _dot(m_kk, v_e) + expand(_dot_nt(kt, st))
        p = -n_mat
        for it in range(6):
            u = u + _dot(p, u)
            if it < 5:
                p = _dot(p, p)
        y = collapse(_dot(m_rk, v_e) - _dot(m_rb, u)) + _dot_nt(rt, st)
        un = collapse(u)
        upd = _dot_tn(v, k2 * e_c) - _dot_tn(un, b * e_c)
        state[gi] = st * jnp.exp(gc) + jnp.where(same, upd, 0.0)
        ys.append(y)
    y = jnp.concatenate(ys, axis=1)
    bd = bd_ref[...]
    mu = _dot_x2(y, bd) * (1.0 / n)
    yc = y - mu
    var = _dot_x2(yc * yc, bd) * (1.0 / n)
    gn = gn_ref[...]
    ygn = yc * lax.rsqrt(var + RWKV_GN_EPS) * gn[0:1] + gn[1:2]
    o_ref[...] = ((ygn + bonus_ref[...].astype(F32)) * g_ref[...].astype(F32)).astype(BF16)


def _rwkv_mixer(x2, mod_l, bsz, seq, mu, w_in, w0, w_w1, w_w2, a0, w_a1, w_a2, w_g1, w_g2, k_k, k_a, r_k,
                gn_g, gn_b):
    ntok = x2.shape[0]
    tiles = ntok // ROW_TILE
    tps = seq // ROW_TILE
    d = D_MODEL
    wr, wk, wv = (w_in[j].astype(BF16) for j in range(3))
    ww1 = _pad_to(w_w1, 1, LANES).astype(BF16)
    ww2 = _pad_to(w_w2, 0, LANES).astype(BF16)
    wa1 = _pad_to(w_a1, 1, LANES).astype(BF16)
    wa2 = _pad_to(w_a2, 0, LANES).astype(BF16)
    wg1 = _pad_to(w_g1, 1, 2 * LANES).astype(BF16)
    wg2 = _pad_to(w_g2, 0, 2 * LANES).astype(BF16)
    vec = _pad_to(jnp.stack([w0, a0, k_k, k_a, r_k.reshape(d)]).astype(F32), 0, 8)
    hid = jnp.arange(d) // RWKV_HEAD_DIM
    bd = (hid[:, None] == hid[None, :]).astype(BF16)
    outs = pl.pallas_call(
        functools.partial(_rwkv_in_kernel, tiles_per_seq=tps),
        grid=(tiles,),
        in_specs=[_row_spec(d), _mod_spec(tps), _full_spec((6, d)), _full_spec((d, d)), _full_spec((d, d)),
                  _full_spec((d, d)), _full_spec(ww1.shape), _full_spec(ww2.shape), _full_spec(wa1.shape),
                  _full_spec(wa2.shape), _full_spec(wg1.shape), _full_spec(wg2.shape), _full_spec((8, d)),
                  _full_spec((d, d))],
        out_specs=[_row_spec(d)] * 8,
        out_shape=[jax.ShapeDtypeStruct((ntok, d), F32 if j == 1 else BF16) for j in range(8)],
        scratch_shapes=[pltpu.VMEM((ROW_TILE + 8, d), F32)],
        compiler_params=_cparams("arbitrary"),
        name="rwkv_in",
    )(x2, mod_l, mu.astype(F32), wr, wk, wv, ww1, ww2, wa1, wa2, wg1, wg2, vec, bd)
    r, lw, k2, v, kk, b, g, bonus = outs

    C = RWKV_CHUNK
    nc = seq // C
    chunk = pl.BlockSpec((C, d), lambda bb, c: (bb * nc + c, 0))
    full2 = lambda shape: pl.BlockSpec(shape, lambda bb, c: (0, 0))
    gn = _pad_to(jnp.stack([gn_g, gn_b]).astype(F32), 0, 8)
    tri = jnp.tril(jnp.ones((C, C), BF16))
    gw = RWKV_GROUP * RWKV_HEAD_DIM
    return pl.pallas_call(
        _rwkv_scan_kernel,
        grid=(bsz, nc),
        in_specs=[chunk] * 8 + [full2((8, d)), full2((C, C)), full2((d, d))],
        out_specs=chunk,
        out_shape=jax.ShapeDtypeStruct((ntok, d), BF16),
        scratch_shapes=[pltpu.VMEM((d // gw, gw, gw), F32)],
        compiler_params=_cparams("parallel", "arbitrary"),
        name="rwkv_scan",
    )(r, lw, k2, v, kk, b, g, bonus, gn, tri, bd)


def _s5_in_kernel(x_ref, mod_ref, w_ref, u_ref):
    h = _modulate(x_ref[...], mod_ref[0], 0, 1)
    u_ref[...] = _dot(h, w_ref[...])


def _s5_scan_kernel(u_ref, kb_ref, ws_ref, wo_ref, lam_ref, y_ref, e_sc, xp_sc, *, bsz):
    u = u_ref[0]
    nchunk = u.shape[0] // bsz
    e_sc[...] = _dot(u, ws_ref[0, 0]) + _dot(u, ws_ref[0, 1])
    lam = lam_ref[0]
    l0 = lam[0:1]
    l1 = lam[1:2]

    def step(i, xst):
        off = pl.multiple_of(i * bsz, bsz)
        xp_sc[pl.ds(off, bsz), :] = xst
        return xst * l0 + pltpu.roll(xst, S5_STATE, axis=1) * l1 + e_sc[pl.ds(off, bsz), :]

    lax.fori_loop(0, nchunk, step, jnp.zeros((bsz, 2 * S5_STATE), F32))
    xh, xl = _split(xp_sc[...])
    y_ref[0] = (_dot(u, kb_ref[0, 0]) + _dot(u, kb_ref[0, 1])
                + _dot_hl(xh, xl, wo_ref[0, 0], wo_ref[0, 1]))


def _s5_glu_kernel(y_ref, u_ref, d_ref, w_ref, o_ref):
    y = y_ref[...] + d_ref[...] * u_ref[...]
    ge = 0.5 * y * (1.0 + jnp.tanh(math.sqrt(2.0 / math.pi) * (y + 0.044715 * (y * y * y))))
    ab = _dot(ge, w_ref[...])
    o_ref[...] = (ab[:, :D_MODEL] * _sigmoid(ab[:, D_MODEL:])).astype(BF16)


def _s5_operators(a_re, a_im, log_dt, b_re, b_im, c_re, c_im):
    hp = lax.Precision.HIGHEST
    L = S5_CHUNK
    a_re, a_im = a_re.astype(F32), a_im.astype(F32)
    dt = jnp.exp(log_dt.astype(F32))[:, None]
    mag = jnp.exp(a_re * dt)
    ab_re, ab_im = mag * jnp.cos(a_im * dt), mag * jnp.sin(a_im * dt)
    den = jnp.square(a_re) + jnp.square(a_im)
    f_re = ((ab_re - 1.0) * a_re + ab_im * a_im) / den
    f_im = (ab_im * a_re - (ab_re - 1.0) * a_im) / den
    b_re, b_im = b_re.astype(F32), b_im.astype(F32)
    bb_re = f_re[..., None] * b_re - f_im[..., None] * b_im
    bb_im = f_re[..., None] * b_im + f_im[..., None] * b_re
    tau = jnp.arange(L + 1, dtype=F32)[:, None, None]
    pmag = jnp.exp(tau * (a_re * dt)[None])
    pw_re = pmag * jnp.cos(tau * (a_im * dt)[None])
    pw_im = pmag * jnp.sin(tau * (a_im * dt)[None])
    c_re, c_im = c_re.astype(F32), c_im.astype(F32)
    cl_re = c_re[None] * pw_re[:, :, None, :] - c_im[None] * pw_im[:, :, None, :]
    cl_im = c_re[None] * pw_im[:, :, None, :] + c_im[None] * pw_re[:, :, None, :]
    kern = (jnp.einsum('tgcp,gpd->tgcd', cl_re, bb_re, precision=hp)
            - jnp.einsum('tgcp,gpd->tgcd', cl_im, bb_im, precision=hp))
    s_idx = jnp.arange(L)[:, None]
    t_idx = jnp.arange(L)[None, :]
    lag = jnp.clip(t_idx - s_idx, 0, L)
    kt = kern[lag]
    kt = jnp.where((t_idx >= s_idx)[:, :, None, None, None], kt, 0.0)
    kbig = jnp.transpose(kt, (2, 0, 4, 1, 3)).reshape(S5_N_GROUPS, L * S5_GROUP, L * S5_GROUP)
    rev_re = pw_re[L - 1 - jnp.arange(L)]
    rev_im = pw_im[L - 1 - jnp.arange(L)]
    st_re = rev_re[:, :, :, None] * bb_re[None] - rev_im[:, :, :, None] * bb_im[None]
    st_im = rev_re[:, :, :, None] * bb_im[None] + rev_im[:, :, :, None] * bb_re[None]
    wst = jnp.concatenate([jnp.transpose(st_re, (1, 0, 3, 2)), jnp.transpose(st_im, (1, 0, 3, 2))], axis=-1)
    wst = wst.reshape(S5_N_GROUPS, L * S5_GROUP, 2 * S5_STATE)
    o_re = jnp.transpose(cl_re[1:], (1, 3, 0, 2)).reshape(S5_N_GROUPS, S5_STATE, L * S5_GROUP)
    o_im = jnp.transpose(cl_im[1:], (1, 3, 0, 2)).reshape(S5_N_GROUPS, S5_STATE, L * S5_GROUP)
    wout = jnp.concatenate([o_re, -o_im], axis=1)
    lam = jnp.stack([jnp.concatenate([pw_re[L], pw_re[L]], -1), jnp.concatenate([-pw_im[L], pw_im[L]], -1)], axis=1)
    hl = lambda w: jnp.stack(_split(w), axis=1)
    return hl(kbig), hl(wst), hl(wout), lam


def _s5_mixer(x2, mod_l, bsz, seq, w_in, a_re, a_im, log_dt, b_re, b_im, c_re, c_im, d_skip, w_glu):
    ntok = x2.shape[0]
    tiles = ntok // ROW_TILE
    tps = seq // ROW_TILE
    d = D_MODEL
    u = pl.pallas_call(
        _s5_in_kernel,
        grid=(tiles,),
        in_specs=[_row_spec(d), _mod_spec(tps), _full_spec((d, d))],
        out_specs=_row_spec(d),
        out_shape=jax.ShapeDtypeStruct((ntok, d), F32),
        compiler_params=_cparams("parallel"),
        name="s5_in",
    )(x2, mod_l, w_in.astype(BF16))

    L = S5_CHUNK
    nchunk = seq // L
    rows = nchunk * bsz
    width = L * S5_GROUP
    kbig, wst, wout, lam = _s5_operators(a_re, a_im, log_dt, b_re, b_im, c_re, c_im)
    ug = jnp.transpose(u.reshape(bsz, nchunk, L, S5_N_GROUPS, S5_GROUP), (3, 1, 0, 2, 4))
    ug = ug.reshape(S5_N_GROUPS, rows, width).astype(BF16)
    grp = lambda *shape: pl.BlockSpec((1,) + shape, lambda g: (g,) + (0,) * len(shape))
    yg = pl.pallas_call(
        functools.partial(_s5_scan_kernel, bsz=bsz),
        grid=(S5_N_GROUPS,),
        in_specs=[grp(rows, width), grp(2, width, width), grp(2, width, 2 * S5_STATE),
                  grp(2, 2 * S5_STATE, width), grp(2, 2 * S5_STATE)],
        out_specs=grp(rows, width),
        out_shape=jax.ShapeDtypeStruct((S5_N_GROUPS, rows, width), F32),
        scratch_shapes=[pltpu.VMEM((rows, 2 * S5_STATE), F32), pltpu.VMEM((rows, 2 * S5_STATE), F32)],
        compiler_params=_cparams("parallel"),
        name="s5_scan",
    )(ug, kbig, wst, wout, lam)
    y = jnp.transpose(yg.reshape(S5_N_GROUPS, nchunk, bsz, L, S5_GROUP), (2, 1, 3, 0, 4)).reshape(ntok, d)

    return pl.pallas_call(
        _s5_glu_kernel,
        grid=(tiles,),
        in_specs=[_row_spec(d), _row_spec(d), _full_spec((1, d)), _full_spec((d, 2 * d))],
        out_specs=_row_spec(d),
        out_shape=jax.ShapeDtypeStruct((ntok, d), BF16),
        compiler_params=_cparams("parallel"),
        name="s5_glu",
    )(y, u, d_skip.astype(F32).reshape(1, d), w_glu.astype(BF16))


def _post_kernel(x_ref, y_ref, w_ref, mod_ref, ln_ref, rw_ref, rb_ref, tris_ref,
                 xn_ref, h2_ref, rout_ref, cnt_ref, run):
    i = pl.program_id(0)
    tm = x_ref.shape[0]

    @pl.when(i == 0)
    def _():
        run[...] = jnp.zeros(run.shape, F32)

    m = mod_ref[0]
    ln = ln_ref[...]
    y = jnp.dot(y_ref[...], w_ref[...], preferred_element_type=F32)
    xn = _layer_norm(DEEPNORM_ALPHA * x_ref[...] + (1.0 + m[2:3]) * y, ln[0:1], ln[1:2])
    xn_ref[...] = xn
    h2 = _modulate(xn, m, 3, 4)
    h2_ref[...] = h2

    hh, hl = _split(h2)
    logits = _dot_nt(rw_ref[0], hh) + _dot_nt(rw_ref[0], hl) + _dot_nt(rw_ref[1], hh)
    ex = jnp.exp(logits - jnp.max(logits, axis=0, keepdims=True))
    probs = ex / jnp.sum(ex, axis=0, keepdims=True)
    sel = probs + rb_ref[...]

    def first_max(vals):
        best = vals[0]
        for v in vals[1:]:
            best = jnp.maximum(best, v)
        idx = jnp.full(best.shape, len(vals) - 1, jnp.int32)
        for j in range(len(vals) - 2, -1, -1):
            idx = jnp.where(vals[j] == best, j, idx)
        return best, idx

    scores, firsts, seconds = [], [], []
    for g in range(N_EXPERT_GROUPS):
        s = [sel[g * EXPERTS_PER_GROUP + j:g * EXPERTS_PER_GROUP + j + 1, :] for j in range(EXPERTS_PER_GROUP)]
        v1, i1 = first_max(s)
        v2, i2 = first_max([jnp.where(i1 == j, NEG_BIG, s[j]) for j in range(EXPERTS_PER_GROUP)])
        scores.append(v1 + v2)
        firsts.append(i1)
        seconds.append(i2)
    _, grp = first_max(scores)
    e0 = jnp.zeros_like(grp)
    e1 = jnp.zeros_like(grp)
    for g in range(N_EXPERT_GROUPS):
        e0 = jnp.where(grp == g, g * EXPERTS_PER_GROUP + firsts[g], e0)
        e1 = jnp.where(grp == g, g * EXPERTS_PER_GROUP + seconds[g], e1)

    eidx = lax.broadcasted_iota(jnp.int32, (N_EXPERTS, tm), 0)
    hit0 = eidx == e0
    hit1 = eidx == e1
    w0 = jnp.sum(jnp.where(hit0, probs, 0.0), axis=0, keepdims=True)
    w1 = jnp.sum(jnp.where(hit1, probs, 0.0), axis=0, keepdims=True)
    wsum = w0 + w1
    onehot = hit0.astype(F32) + hit1.astype(F32)
    before = _dot(onehot, tris_ref[...]) + run[...]
    r0 = jnp.sum(jnp.where(hit0, before, 0.0), axis=0, keepdims=True)
    r1 = jnp.sum(jnp.where(hit1, before, 0.0), axis=0, keepdims=True)
    run[...] = run[...] + jnp.sum(onehot, axis=1, keepdims=True)
    cnt_ref[...] = jnp.broadcast_to(run[...], cnt_ref.shape)

    rid = lax.broadcasted_iota(jnp.int32, (8, tm), 0)
    rows = (e0.astype(F32), e1.astype(F32), w0 / wsum, w1 / wsum, r0, r1)
    out = jnp.zeros((8, tm), F32)
    for j, val in enumerate(rows):
        out = jnp.where(rid == j, val, out)
    rout_ref[...] = out


def _post(x2, y_in, w_out, mod_l, ln_g, ln_b, rw, rb, seq):
    ntok = x2.shape[0]
    tiles = ntok // ROW_TILE
    tps = seq // ROW_TILE
    d = D_MODEL
    kin = y_in.shape[1]
    ln = _pad_to(jnp.stack([ln_g, ln_b]).astype(F32), 0, 8)
    tris = jnp.triu(jnp.ones((ROW_TILE, ROW_TILE), BF16), 1)
    return pl.pallas_call(
        _post_kernel,
        grid=(tiles,),
        in_specs=[_row_spec(d), _row_spec(kin), _full_spec((kin, d)), _mod_spec(tps), _full_spec((8, d)),
                  _full_spec((2, N_EXPERTS, d)), _full_spec((N_EXPERTS, 1)), _full_spec((ROW_TILE, ROW_TILE))],
        out_specs=[_row_spec(d), _row_spec(d), pl.BlockSpec((8, ROW_TILE), lambda i: (0, i)),
                   _full_spec((N_EXPERTS, LANES))],
        out_shape=[jax.ShapeDtypeStruct((ntok, d), F32), jax.ShapeDtypeStruct((ntok, d), F32),
                   jax.ShapeDtypeStruct((8, ntok), F32), jax.ShapeDtypeStruct((N_EXPERTS, LANES), F32)],
        scratch_shapes=[pltpu.VMEM((N_EXPERTS, 1), F32)],
        compiler_params=_cparams("arbitrary"),
        name="post",
    )(x2, y_in, w_out.astype(BF16), mod_l, ln, rw, rb, tris)


def _dispatch_kernel(dest_ref, h_ref, zero_ref, xs_ref, sem):
    del zero_ref
    tm = h_ref.shape[0]

    def copy(t, slot):
        return pltpu.make_async_copy(h_ref.at[pl.ds(t, 1)], xs_ref.at[pl.ds(dest_ref[0, 0, slot * tm + t], 1)],
                                     sem.at[0])

    def issue(t, carry):
        copy(t, 0).start()
        copy(t, 1).start()
        return carry

    def drain(t, carry):
        copy(t, 0).wait()
        copy(t, 1).wait()
        return carry

    lax.fori_loop(0, tm, issue, 0)
    lax.fori_loop(0, tm, drain, 0)


def _ffn_kernel(be_ref, nu_ref, x_ref, wu_ref, wd_ref, o_ref, wub, wdb):
    i = pl.program_id(0)
    prev = be_ref[jnp.maximum(i - 1, 0)]
    rows = 128

    @pl.when((i == 0) | (be_ref[i] != prev))
    def _():
        def cast(j, carry):
            sl = pl.ds(pl.multiple_of(j * rows, rows), rows)
            wub[sl, :] = wu_ref[0, sl, :].astype(BF16)
            wdb[sl, :] = wd_ref[0, sl, :].astype(BF16)
            return carry
        lax.fori_loop(0, D_MODEL // rows, cast, 0)

    @pl.when(i < nu_ref[0])
    def _():
        hid = jnp.dot(x_ref[...].astype(BF16), wub[...], preferred_element_type=F32)
        act = _silu(hid[:, :D_MODEL]) * hid[:, D_MODEL:]
        o_ref[...] = jnp.dot(act.astype(BF16), wdb[...], preferred_element_type=F32)

    @pl.when(i >= nu_ref[0])
    def _():
        o_ref[...] = jnp.zeros(o_ref.shape, F32)


def _combine_kernel(dest_ref, xn_ref, wc_ref, mod_ref, ln_ref, ys_ref, o_ref, buf0, buf1, sem):
    tm = xn_ref.shape[0]
    bufs = (buf0, buf1)

    def copy(t, slot):
        return pltpu.make_async_copy(ys_ref.at[pl.ds(dest_ref[0, 0, slot * tm + t], 1)], bufs[slot].at[pl.ds(t, 1)],
                                     sem.at[0])

    def issue(t, carry):
        copy(t, 0).start()
        copy(t, 1).start()
        return carry

    def drain(t, carry):
        copy(t, 0).wait()
        copy(t, 1).wait()
        return carry

    lax.fori_loop(0, tm, issue, 0)
    lax.fori_loop(0, tm, drain, 0)
    m = mod_ref[0]
    ln = ln_ref[...]
    wc = wc_ref[...]
    y = wc[:, 0:1] * buf0[...] + wc[:, 1:2] * buf1[...]
    o_ref[...] = _layer_norm(DEEPNORM_ALPHA * xn_ref[...] + (1.0 + m[5:6]) * y, ln[0:1], ln[1:2])


def _moe(xn, h2, rout, cnt, mod_l, ln_g, ln_b, w_up, w_down, seq):
    ntok = xn.shape[0]
    d = D_MODEL
    tiles = ntok // ROW_TILE
    tps = seq // ROW_TILE
    bm = MOE_ROWS
    n_assign = ntok * TOP_K
    n_blocks = n_assign // bm + N_EXPERTS
    n_pad = n_blocks * bm

    e0 = rout[0].astype(jnp.int32)
    e1 = rout[1].astype(jnp.int32)
    counts = cnt[:, 0].astype(jnp.int32)
    padded = (counts + bm - 1) // bm * bm
    pad_end = jnp.cumsum(padded)
    pad_start = pad_end - padded
    dest0 = pad_start[e0] + rout[4].astype(jnp.int32)
    dest1 = pad_start[e1] + rout[5].astype(jnp.int32)
    dest = jnp.concatenate([dest0.reshape(tiles, 1, ROW_TILE), dest1.reshape(tiles, 1, ROW_TILE)], axis=-1)
    block_expert = jnp.minimum(jnp.searchsorted(pad_end, jnp.arange(n_blocks) * bm, side='right'),
                               N_EXPERTS - 1).astype(jnp.int32)
    n_used = (pad_end[-1] // bm).astype(jnp.int32).reshape(1)
    wcol = jnp.transpose(rout[2:4])

    dest_spec = pl.BlockSpec((1, 1, 2 * ROW_TILE), lambda i: (i, 0, 0), memory_space=pltpu.SMEM)
    xs = pl.pallas_call(
        _dispatch_kernel,
        grid=(tiles,),
        in_specs=[dest_spec, _row_spec(d), pl.BlockSpec(memory_space=pl.ANY)],
        out_specs=pl.BlockSpec(memory_space=pl.ANY),
        out_shape=jax.ShapeDtypeStruct((n_pad, d), F32),
        scratch_shapes=[pltpu.SemaphoreType.DMA((1,))],
        input_output_aliases={2: 0},
        compiler_params=_cparams("arbitrary"),
        name="moe_dispatch",
    )(dest, h2, jnp.zeros((n_pad, d), F32))

    ys = pl.pallas_call(
        _ffn_kernel,
        grid_spec=pltpu.PrefetchScalarGridSpec(
            num_scalar_prefetch=2,
            grid=(n_blocks,),
            in_specs=[pl.BlockSpec((bm, d), lambda i, be, nu: (i, 0)),
                      pl.BlockSpec((1, d, 2 * d), lambda i, be, nu: (be[i], 0, 0)),
                      pl.BlockSpec((1, d, d), lambda i, be, nu: (be[i], 0, 0))],
            out_specs=pl.BlockSpec((bm, d), lambda i, be, nu: (i, 0)),
            scratch_shapes=[pltpu.VMEM((d, 2 * d), BF16), pltpu.VMEM((d, d), BF16)],
        ),
        out_shape=jax.ShapeDtypeStruct((n_pad, d), F32),
        compiler_params=_cparams("arbitrary"),
        name="moe_ffn",
    )(block_expert, n_used, xs, w_up, w_down)

    ln = _pad_to(jnp.stack([ln_g, ln_b]).astype(F32), 0, 8)
    return pl.pallas_call(
        _combine_kernel,
        grid=(tiles,),
        in_specs=[dest_spec, _row_spec(d), pl.BlockSpec((ROW_TILE, 2), lambda i: (i, 0)), _mod_spec(tps),
                  _full_spec((8, d)), pl.BlockSpec(memory_space=pl.ANY)],
        out_specs=_row_spec(d),
        out_shape=jax.ShapeDtypeStruct((ntok, d), F32),
        scratch_shapes=[pltpu.VMEM((ROW_TILE, d), F32), pltpu.VMEM((ROW_TILE, d), F32),
                        pltpu.SemaphoreType.DMA((1,))],
        compiler_params=_cparams("arbitrary"),
        name="moe_combine",
    )(dest, xn, wcol, mod_l, ln, ys)


def kernel(x, c, ada_w, ada_b, ln1_g, ln1_b, ln2_g, ln2_b, ssd_w_in, ssd_conv_w, ssd_conv_b, ssd_dt_bias, ssd_a_log, ssd_d, ssd_norm_g, ssd_w_out, gla_w_in, gla_w_gate, gla_b_gate, gla_norm_g, gla_w_out, rwkv_mu, rwkv_w_in, rwkv_w0, rwkv_w_w1, rwkv_w_w2, rwkv_a0, rwkv_w_a1, rwkv_w_a2, rwkv_w_g1, rwkv_w_g2, rwkv_k_k, rwkv_k_a, rwkv_r_k, rwkv_gn_g, rwkv_gn_b, rwkv_w_out, s5_w_in, s5_a_re, s5_a_im, s5_log_dt, s5_b_re, s5_b_im, s5_c_re, s5_c_im, s5_d, s5_w_glu, s5_w_out, moe_w_up, moe_w_down, router_w, router_b):
    bsz, seq, d = x.shape
    assert d == D_MODEL and seq % ROW_TILE == 0 and seq % SSD_CHUNK == 0
    depth = ada_w.shape[0]
    mod = _adaln_mod(c, ada_w, ada_b).reshape(depth, bsz, 6, d)
    rw = _hilo(jnp.transpose(router_w))
    rb = router_b.astype(F32).reshape(N_EXPERTS, 1)
    x2 = x.reshape(bsz * seq, d)
    for i in range(depth):
        kind, j = i % 4, i // 4
        mod_l = mod[i]
        if kind == 0:
            y = _ssd_mixer(x2, mod_l, bsz, seq, ssd_w_in[j], ssd_conv_w[j], ssd_conv_b[j], ssd_dt_bias[j],
                           ssd_a_log[j], ssd_d[j], ssd_norm_g[j])
            w_out = ssd_w_out[j]
        elif kind == 1:
            y = _gla_mixer(x2, mod_l, bsz, seq, gla_w_in[j], gla_w_gate[j], gla_b_gate[j], gla_norm_g[j])
            w_out = gla_w_out[j]
        elif kind == 2:
            y = _rwkv_mixer(x2, mod_l, bsz, seq, rwkv_mu[j], rwkv_w_in[j], rwkv_w0[j], rwkv_w_w1[j], rwkv_w_w2[j],
                            rwkv_a0[j], rwkv_w_a1[j], rwkv_w_a2[j], rwkv_w_g1[j], rwkv_w_g2[j], rwkv_k_k[j],
                            rwkv_k_a[j], rwkv_r_k[j], rwkv_gn_g[j], rwkv_gn_b[j])
            w_out = rwkv_w_out[j]
        else:
            y = _s5_mixer(x2, mod_l, bsz, seq, s5_w_in[j], s5_a_re[j], s5_a_im[j], s5_log_dt[j], s5_b_re[j],
                          s5_b_im[j], s5_c_re[j], s5_c_im[j], s5_d[j], s5_w_glu[j])
            w_out = s5_w_out[j]
        xn, h2, rout, cnt = _post(x2, y, w_out, mod_l, ln1_g[i], ln1_b[i], rw, rb, seq)
        x2 = _moe(xn, h2, rout, cnt, mod_l, ln2_g[i], ln2_b[i], moe_w_up[i], moe_w_down[i], seq)
    return x2.reshape(bsz, seq, d)
```

```python
import functools
import math

import jax
import jax.numpy as jnp
from jax import lax
from jax.experimental import pallas as pl
from jax.experimental.pallas import tpu as pltpu

F32 = jnp.float32
BF16 = jnp.bfloat16

D_MODEL = 1024
DEPTH = 4
DEEPNORM_ALPHA = (2 * DEPTH) ** 0.25
LN_EPS = 1e-5
RMS_EPS = 1e-5

SSD_D_INNER = 2048
SSD_HEAD_DIM = 64
SSD_N_HEADS = 32
SSD_N_GROUPS = 4
SSD_D_STATE = 128
SSD_CONV = 4
SSD_CONV_DIM = SSD_D_INNER + 2 * SSD_N_GROUPS * SSD_D_STATE
SSD_CHUNK = 128

GLA_N_HEADS = 4
GLA_D_K = 512
GLA_D_V = 1024
GLA_GATE_RANK = 16
GLA_TAU = 16.0
GLA_SUB = 32
GLA_ROWS = 256

RWKV_HEAD_DIM = 64
RWKV_N_HEADS = 16
RWKV_GN_EPS = 64e-5
RWKV_CHUNK = 64
RWKV_GROUP = 4

S5_GROUP = 16
S5_N_GROUPS = 64
S5_STATE = 64
S5_CHUNK = 16

N_EXPERTS = 16
N_EXPERT_GROUPS = 4
EXPERTS_PER_GROUP = 4
TOP_K = 2
MOE_ROWS = 256

ROW_TILE = 256
LANES = 128
VMEM_LIMIT = 56 * 1024 * 1024
NEG_BIG = -1e30


def _cparams(*sem):
    return pltpu.CompilerParams(dimension_semantics=sem, vmem_limit_bytes=VMEM_LIMIT)


def _dot(a, b):
    return jnp.dot(a.astype(BF16), b.astype(BF16), preferred_element_type=F32)


def _dot_nt(a, b):
    return lax.dot_general(a.astype(BF16), b.astype(BF16), (((1,), (1,)), ((), ())), preferred_element_type=F32)


def _dot_tn(a, b):
    return lax.dot_general(a.astype(BF16), b.astype(BF16), (((0,), (0,)), ((), ())), preferred_element_type=F32)


def _split(x):
    hi = x.astype(BF16)
    lo = (x - hi.astype(F32)).astype(BF16)
    return hi, lo


def _split3(x):
    h1 = x.astype(BF16)
    r1 = x - h1.astype(F32)
    h2 = r1.astype(BF16)
    h3 = (r1 - h2.astype(F32)).astype(BF16)
    return h1, h2, h3


def _dot_x2(a, b_exact):
    hi, lo = _split(a)
    return _dot(hi, b_exact) + _dot(lo, b_exact)


def _dot_x3(a, b_exact):
    h1, h2, h3 = _split3(a)
    return _dot(h1, b_exact) + _dot(h2, b_exact) + _dot(h3, b_exact)


def _dot_hl(a_hi, a_lo, b_hi, b_lo):
    return _dot(a_hi, b_hi) + _dot(a_lo, b_hi) + _dot(a_hi, b_lo)


def _sigmoid(x):
    return 1.0 / (1.0 + jnp.exp(-x))


def _silu(x):
    return x * _sigmoid(x)


def _softplus(x):
    return jnp.maximum(x, 0.0) + jnp.log1p(jnp.exp(-jnp.abs(x)))


def _hilo(w):
    hi, lo = _split(w.astype(F32))
    return jnp.stack([hi, lo])


def _pad_to(a, axis, size):
    pad = [(0, 0)] * a.ndim
    pad[axis] = (0, size - a.shape[axis])
    return jnp.pad(a, pad)


def _modulate(x, m, shift_row, scale_row):
    return x * (1.0 + m[scale_row:scale_row + 1]) + m[shift_row:shift_row + 1]


def _layer_norm(v, g, b):
    mu = jnp.mean(v, axis=-1, keepdims=True)
    vc = v - mu
    var = jnp.mean(vc * vc, axis=-1, keepdims=True)
    return vc * lax.rsqrt(var + LN_EPS) * g + b


def _mod_kernel(c_ref, w_ref, b_ref, o_ref):
    o_ref[0] = _dot(_silu(c_ref[...]), w_ref[0]) + b_ref[0]


def _adaln_mod(c, ada_w, ada_b):
    nl, d, n6 = ada_w.shape
    bsz = c.shape[0]
    tn = 512
    return pl.pallas_call(
        _mod_kernel,
        grid=(nl, n6 // tn),
        in_specs=[
            pl.BlockSpec((bsz, d), lambda l, j: (0, 0)),
            pl.BlockSpec((1, d, tn), lambda l, j: (l, 0, j)),
            pl.BlockSpec((1, 1, tn), lambda l, j: (l, 0, j)),
        ],
        out_specs=pl.BlockSpec((1, bsz, tn), lambda l, j: (l, 0, j)),
        out_shape=jax.ShapeDtypeStruct((nl, bsz, n6), F32),
        compiler_params=_cparams("parallel", "parallel"),
        name="adaln_mod",
    )(c, ada_w, ada_b.reshape(nl, 1, n6))


def _row_spec(width, tm=ROW_TILE):
    return pl.BlockSpec((tm, width), lambda i: (i, 0))


def _full_spec(shape):
    nd = len(shape)
    return pl.BlockSpec(shape, lambda *_: (0,) * nd)


def _mod_spec(tiles_per_seq):
    return pl.BlockSpec((1, 6, D_MODEL), lambda i: (i // tiles_per_seq, 0, 0))


def _ssd_in_kernel(x_ref, mod_ref, wz_ref, wx_ref, wdt_ref, z_ref, xbc_ref, dt_ref):
    h = _modulate(x_ref[...], mod_ref[0], 0, 1)
    hb, hl = _split(h)
    z_ref[...] = jnp.dot(hb, wz_ref[...], preferred_element_type=F32).astype(BF16)
    xbc_ref[...] = jnp.dot(hb, wx_ref[...], preferred_element_type=F32)
    dt_ref[...] = _dot_hl(hb, hl, wdt_ref[0], wdt_ref[1])


def _ssd_scan_kernel(z_ref, xbc_ref, dt_ref, cw_ref, cb_ref, dtb_ref, an_ref, dsk_ref, ng_ref,
                     e_ref, tri_ref, triu_ref, o_ref, xpad, state):
    c = pl.program_id(1)
    L = z_ref.shape[0]
    nstate = SSD_D_STATE

    @pl.when(c == 0)
    def _():
        state[...] = jnp.zeros(state.shape, F32)
        xpad[0:8, :] = jnp.zeros((8, SSD_CONV_DIM), F32)

    @pl.when(c != 0)
    def _():
        xpad[0:8, :] = xpad[L:L + 8, :]

    xpad[8:8 + L, :] = xbc_ref[...]

    acc = cb_ref[...] + cw_ref[3:4, :] * xpad[8:8 + L, :]
    for j in range(1, SSD_CONV):
        acc = acc + cw_ref[3 - j:4 - j, :] * xpad[pl.ds(8 - j, L), :]
    act = _silu(acc)
    xs = act[:, :SSD_D_INNER]
    bm = act[:, SSD_D_INNER:SSD_D_INNER + SSD_N_GROUPS * nstate]
    cm = act[:, SSD_D_INNER + SSD_N_GROUPS * nstate:]

    dtv = _softplus(dt_ref[...] + dtb_ref[...])
    dta = dtv * an_ref[...]
    dhi, dlo = _split(dta)
    tri = tri_ref[...]
    triu = triu_ref[...]
    acum = _dot(tri, dhi) + _dot(tri, dlo)
    acum_t = _dot_tn(dhi, triu) + _dot_tn(dlo, triu)

    e = e_ref[...]
    acum_x = _dot_x3(acum, e)
    dt_x = _dot_x3(dtv, e)
    eac_x = jnp.exp(acum_x)
    last_x = acum_x[L - 1:L, :]
    elast_x = jnp.exp(last_x)
    xdt = xs * dt_x
    xw = xs * (jnp.exp(last_x - acum_x) * dt_x)

    row = lax.broadcasted_iota(jnp.int32, (L, L), 0)
    col = lax.broadcasted_iota(jnp.int32, (L, L), 1)
    causal = row >= col
    lane = lax.broadcasted_iota(jnp.int32, (L, LANES), 1)
    left = lane < SSD_HEAD_DIM

    pieces = []
    heads_per_group = SSD_N_HEADS // SSD_N_GROUPS
    for g in range(SSD_N_GROUPS):
        bg = bm[:, g * nstate:(g + 1) * nstate]
        cg = cm[:, g * nstate:(g + 1) * nstate]
        cb = _dot_nt(cg, bg)
        for p in range(heads_per_group // 2):
            h0 = g * heads_per_group + 2 * p
            pair = h0 // 2
            sl = slice(h0 * SSD_HEAD_DIM, h0 * SSD_HEAD_DIM + LANES)
            ms = []
            for hh in (h0, h0 + 1):
                seg = acum[:, hh:hh + 1] - acum_t[hh:hh + 1, :]
                ms.append((cb * jnp.exp(jnp.where(causal, seg, NEG_BIG))).astype(BF16))
            mcat = jnp.concatenate(ms, axis=1)
            xp = xdt[:, sl]
            xbd = jnp.concatenate([jnp.where(left, xp, 0.0), jnp.where(left, 0.0, xp)], axis=0)
            st = state[pair]
            y_pair = _dot(mcat, xbd) + _dot(cg, st) * eac_x[:, sl]
            state[pair] = st * elast_x[:, sl] + _dot_tn(bg, xw[:, sl])
            pieces.append(y_pair)
    y = jnp.concatenate(pieces, axis=1) + dsk_ref[...] * xs
    y = y * _silu(z_ref[...].astype(F32))
    gw = SSD_D_INNER // SSD_N_GROUPS
    outs = []
    for g in range(SSD_N_GROUPS):
        yg = y[:, g * gw:(g + 1) * gw]
        outs.append(yg * lax.rsqrt(jnp.mean(yg * yg, axis=-1, keepdims=True) + RMS_EPS))
    o_ref[...] = (jnp.concatenate(outs, axis=1) * ng_ref[...]).astype(BF16)


def _ssd_mixer(x2, mod_l, bsz, seq, w_in, conv_w, conv_b, dt_bias, a_log, d_skip, norm_g):
    ntok = x2.shape[0]
    tiles = ntok // ROW_TILE
    tps = seq // ROW_TILE
    wz = w_in[:, :SSD_D_INNER].astype(BF16)
    wx = w_in[:, SSD_D_INNER:SSD_D_INNER + SSD_CONV_DIM].astype(BF16)
    wdt = _hilo(_pad_to(w_in[:, SSD_D_INNER + SSD_CONV_DIM:], 1, LANES))
    z, xbc, dt = pl.pallas_call(
        _ssd_in_kernel,
        grid=(tiles,),
        in_specs=[_row_spec(D_MODEL), _mod_spec(tps), _full_spec(wz.shape), _full_spec(wx.shape),
                  _full_spec(wdt.shape)],
        out_specs=[_row_spec(SSD_D_INNER), _row_spec(SSD_CONV_DIM), _row_spec(LANES)],
        out_shape=[jax.ShapeDtypeStruct((ntok, SSD_D_INNER), BF16),
                   jax.ShapeDtypeStruct((ntok, SSD_CONV_DIM), F32),
                   jax.ShapeDtypeStruct((ntok, LANES), F32)],
        compiler_params=_cparams("parallel"),
        name="ssd_in",
    )(x2, mod_l, wz, wx, wdt)

    L = SSD_CHUNK
    nc = seq // L
    a_neg = _pad_to((-jnp.exp(a_log.astype(F32))).reshape(1, SSD_N_HEADS), 1, LANES)
    dtb = _pad_to(dt_bias.astype(F32).reshape(1, SSD_N_HEADS), 1, LANES)
    dsk = jnp.repeat(d_skip.astype(F32), SSD_HEAD_DIM).reshape(1, SSD_D_INNER)
    expand = (jnp.arange(LANES)[:, None] == (jnp.arange(SSD_D_INNER) // SSD_HEAD_DIM)[None, :]).astype(BF16)
    tri = jnp.tril(jnp.ones((L, L), BF16))
    triu = jnp.triu(jnp.ones((L, L), BF16))
    chunk = lambda w: pl.BlockSpec((L, w), lambda b, c: (b * nc + c, 0))
    full2 = lambda shape: pl.BlockSpec(shape, lambda b, c: (0, 0))
    return pl.pallas_call(
        _ssd_scan_kernel,
        grid=(bsz, nc),
        in_specs=[chunk(SSD_D_INNER), chunk(SSD_CONV_DIM), chunk(LANES),
                  full2((SSD_CONV, SSD_CONV_DIM)), full2((1, SSD_CONV_DIM)), full2((1, LANES)), full2((1, LANES)),
                  full2((1, SSD_D_INNER)), full2((1, SSD_D_INNER)), full2((LANES, SSD_D_INNER)),
                  full2((L, L)), full2((L, L))],
        out_specs=chunk(SSD_D_INNER),
        out_shape=jax.ShapeDtypeStruct((ntok, SSD_D_INNER), BF16),
        scratch_shapes=[pltpu.VMEM((L + 8, SSD_CONV_DIM), F32),
                        pltpu.VMEM((SSD_N_HEADS // 2, SSD_D_STATE, LANES), F32)],
        compiler_params=_cparams("parallel", "arbitrary"),
        name="ssd_scan",
    )(z, xbc, dt, conv_w.astype(F32), conv_b.astype(F32).reshape(1, -1), dtb, a_neg, dsk,
      norm_g.astype(F32).reshape(1, -1), expand, tri, triu)


def _gla_in_kernel(x_ref, mod_ref, wq_ref, wk_ref, wv_ref, wr_ref, wg_ref, wgate_ref, bgate_ref,
                   q_ref, k_ref, v_ref, r_ref, la_ref):
    h = _modulate(x_ref[...], mod_ref[0], 0, 1)
    hb, hl = _split(h)
    dk = GLA_D_K // GLA_N_HEADS
    q_ref[...] = jnp.dot(hb, wq_ref[...], preferred_element_type=F32) * (dk ** -0.5)
    k_ref[...] = jnp.dot(hb, wk_ref[...], preferred_element_type=F32)
    v_ref[...] = jnp.dot(hb, wv_ref[...], preferred_element_type=F32)
    r_ref[...] = jnp.dot(hb, wr_ref[...], preferred_element_type=F32).astype(BF16)
    g_low = _dot_hl(hb, hl, wg_ref[0], wg_ref[1])
    gh, gl = _split(g_low)
    gz = _dot_hl(gh, gl, wgate_ref[0], wgate_ref[1]) + bgate_ref[...]
    la_ref[...] = (jnp.minimum(gz, 0.0) - jnp.log1p(jnp.exp(-jnp.abs(gz)))) * (1.0 / GLA_TAU)


def _gla_scan_kernel(q_ref, k_ref, v_ref, r_ref, la_ref, ng_ref, tri_ref, o_ref, state):
    c = pl.program_id(1)
    rows = q_ref.shape[0]
    sub = GLA_SUB
    nsub = rows // sub
    dk = GLA_D_K // GLA_N_HEADS
    dv = GLA_D_V // GLA_N_HEADS

    @pl.when(c == 0)
    def _():
        state[...] = jnp.zeros(state.shape, F32)

    q = q_ref[...]
    k = k_ref[...]
    v = v_ref[...]
    lhi, llo = _split(la_ref[...])
    tri = tri_ref[...]
    bcum = _dot(tri, lhi) + _dot(tri, llo)

    def sub_row(a, j):
        a3 = a.reshape(nsub, sub, a.shape[-1])
        return jnp.broadcast_to(a3[:, j:j + 1, :], a3.shape).reshape(a.shape)

    last = sub_row(bcum, sub - 1)
    qdec = q * jnp.exp(bcum)
    kdec = k * jnp.exp(last - bcum)
    elast = jnp.exp(last)

    lane = lax.broadcasted_iota(jnp.int32, (rows, LANES), 1)
    local_row = lax.broadcasted_iota(jnp.int32, (rows, LANES), 0) & (sub - 1)
    att = jnp.zeros((rows, LANES), F32)
    for h in range(GLA_N_HEADS):
        ks = slice(h * dk, (h + 1) * dk)
        qh, kh, bh = q[:, ks], k[:, ks], bcum[:, ks]
        for j in range(sub):
            ex = jnp.exp(jnp.minimum(bh - sub_row(bh, j), 0.0))
            a = jnp.sum(qh * sub_row(kh, j) * ex, axis=-1, keepdims=True)
            att = jnp.where(lane == h * sub + j, a, att)
    att = jnp.where((lane & (sub - 1)) <= local_row, att, 0.0)
    ahi, alo = _split(att)

    ng = ng_ref[...]
    value_head = lax.broadcasted_iota(jnp.int32, (sub, GLA_D_V), 1) >> (dv.bit_length() - 1)
    for s in range(nsub):
        rs = slice(s * sub, (s + 1) * sub)
        vs = v[rs]
        vbd = jnp.concatenate([jnp.where(value_head == h, vs, 0.0) for h in range(GLA_N_HEADS)], axis=0)
        o_intra = _dot(ahi[rs], vbd) + _dot(alo[rs], vbd)
        outs = []
        for h in range(GLA_N_HEADS):
            ks = slice(h * dk, (h + 1) * dk)
            st = state[h]
            oh = o_intra[:, h * dv:(h + 1) * dv] + _dot_nt(qdec[rs, ks], st)
            state[h] = st * elast[s * sub:s * sub + 1, ks] + _dot_tn(vs[:, h * dv:(h + 1) * dv], kdec[rs, ks])
            outs.append(oh * lax.rsqrt(jnp.mean(oh * oh, axis=-1, keepdims=True) + RMS_EPS))
        on = jnp.concatenate(outs, axis=1) * ng
        o_ref[rs, :] = (on * _silu(r_ref[rs, :].astype(F32))).astype(BF16)


def _gla_mixer(x2, mod_l, bsz, seq, w_in, w_gate, b_gate, norm_g):
    ntok = x2.shape[0]
    tiles = ntok // ROW_TILE
    tps = seq // ROW_TILE
    o1, o2, o3 = GLA_D_K, 2 * GLA_D_K, 2 * GLA_D_K + GLA_D_V
    o4 = o3 + GLA_D_V
    wq, wk = w_in[:, :o1].astype(BF16), w_in[:, o1:o2].astype(BF16)
    wv, wr = w_in[:, o2:o3].astype(BF16), w_in[:, o3:o4].astype(BF16)
    wg = _hilo(_pad_to(w_in[:, o4:], 1, LANES))
    wgate = _hilo(_pad_to(w_gate, 0, LANES))
    q, k, v, r, la = pl.pallas_call(
        _gla_in_kernel,
        grid=(tiles,),
        in_specs=[_row_spec(D_MODEL), _mod_spec(tps), _full_spec(wq.shape), _full_spec(wk.shape),
                  _full_spec(wv.shape), _full_spec(wr.shape), _full_spec(wg.shape), _full_spec(wgate.shape),
                  _full_spec((1, GLA_D_K))],
        out_specs=[_row_spec(GLA_D_K), _row_spec(GLA_D_K), _row_spec(GLA_D_V), _row_spec(GLA_D_V),
                   _row_spec(GLA_D_K)],
        out_shape=[jax.ShapeDtypeStruct((ntok, GLA_D_K), F32), jax.ShapeDtypeStruct((ntok, GLA_D_K), F32),
                   jax.ShapeDtypeStruct((ntok, GLA_D_V), F32), jax.ShapeDtypeStruct((ntok, GLA_D_V), BF16),
                   jax.ShapeDtypeStruct((ntok, GLA_D_K), F32)],
        compiler_params=_cparams("parallel"),
        name="gla_in",
    )(x2, mod_l, wq, wk, wv, wr, wg, wgate, b_gate.astype(F32).reshape(1, -1))

    rows = GLA_ROWS
    nc = seq // rows
    chunk = lambda w: pl.BlockSpec((rows, w), lambda b, c: (b * nc + c, 0))
    full2 = lambda shape: pl.BlockSpec(shape, lambda b, c: (0, 0))
    idx = jnp.arange(rows)
    tri = ((idx[:, None] >= idx[None, :]) & (idx[:, None] // GLA_SUB == idx[None, :] // GLA_SUB)).astype(BF16)
    return pl.pallas_call(
        _gla_scan_kernel,
        grid=(bsz, nc),
        in_specs=[chunk(GLA_D_K), chunk(GLA_D_K), chunk(GLA_D_V), chunk(GLA_D_V), chunk(GLA_D_K),
                  full2((1, GLA_D_V)), full2((rows, rows))],
        out_specs=chunk(GLA_D_V),
        out_shape=jax.ShapeDtypeStruct((ntok, GLA_D_V), BF16),
        scratch_shapes=[pltpu.VMEM((GLA_N_HEADS, GLA_D_V // GLA_N_HEADS, GLA_D_K // GLA_N_HEADS), F32)],
        compiler_params=_cparams("parallel", "arbitrary"),
        name="gla_scan",
    )(q, k, v, r, la, norm_g.astype(F32).reshape(1, -1), tri)


def _rwkv_in_kernel(x_ref, mod_ref, mu_ref, wr_ref, wk_ref, wv_ref, ww1_ref, ww2_ref, wa1_ref, wa2_ref,
                    wg1_ref, wg2_ref, vec_ref, bd_ref,
                    r_ref, lw_ref, k2_ref, v_ref, kk_ref, b_ref, g_ref, bonus_ref, hp, *, tiles_per_seq):
    i = pl.program_id(0)
    tm = x_ref.shape[0]
    h = _modulate(x_ref[...], mod_ref[0], 0, 1)

    @pl.when(i % tiles_per_seq == 0)
    def _():
        hp[0:8, :] = jnp.zeros((8, D_MODEL), F32)

    @pl.when(i % tiles_per_seq != 0)
    def _():
        hp[0:8, :] = hp[tm:tm + 8, :]

    hp[8:8 + tm, :] = h
    xx = hp[pl.ds(7, tm), :] - h
    mu = mu_ref[...]
    xr, xw, xk, xv, xa, xg = (h + xx * mu[j:j + 1] for j in range(6))
    vec = vec_ref[...]
    w0, a0, k_k, k_a, r_k = (vec[j:j + 1] for j in range(5))
    bd = bd_ref[...]

    r = _dot(xr, wr_ref[...])
    k = _dot(xk, wk_ref[...])
    v = _dot(xv, wv_ref[...])
    wlog = -_softplus(-(w0 + _dot(jnp.tanh(_dot(xw, ww1_ref[...])), ww2_ref[...]))) - 0.5
    a = _sigmoid(a0 + _dot(_dot(xa, wa1_ref[...]), wa2_ref[...]))
    g = _dot(_sigmoid(_dot(xg, wg1_ref[...])), wg2_ref[...])
    kx = k * k_k
    norm = jnp.sqrt(_dot_x2(kx * kx, bd))
    kk = kx / jnp.maximum(norm, 1e-12)
    k2 = k * (1.0 + (a - 1.0) * k_a)
    bonus = _dot_x2(r * k2 * r_k, bd) * v

    r_ref[...] = r.astype(BF16)
    lw_ref[...] = -jnp.exp(wlog)
    k2_ref[...] = k2.astype(BF16)
    v_ref[...] = v.astype(BF16)
    kk_ref[...] = kk.astype(BF16)
    b_ref[...] = (kk * a).astype(BF16)
    g_ref[...] = g.astype(BF16)
    bonus_ref[...] = bonus.astype(BF16)


def _rwkv_scan_kernel(r_ref, lw_ref, k2_ref, v_ref, kk_ref, b_ref, g_ref, bonus_ref, gn_ref, tri_ref, bd_ref,
                      o_ref, state):
    c = pl.program_id(1)
    C = RWKV_CHUNK
    n = RWKV_HEAD_DIM
    gw = RWKV_GROUP * n
    ngroups = D_MODEL // gw

    @pl.when(c == 0)
    def _():
        state[...] = jnp.zeros(state.shape, F32)

    tri = tri_ref[...]
    ri = lax.broadcasted_iota(jnp.int32, (gw, gw), 0)
    ci = lax.broadcasted_iota(jnp.int32, (gw, gw), 1)
    shift = n.bit_length() - 1
    same = (ri >> shift) == (ci >> shift)
    strict = same & ((ri & (C - 1)) > (ci & (C - 1)))
    incl = same & ((ri & (C - 1)) >= (ci & (C - 1)))
    lane_head = lax.broadcasted_iota(jnp.int32, (C, gw), 1) >> shift

    def expand(a):
        return jnp.concatenate([jnp.where(lane_head == hh, a, 0.0) for hh in range(RWKV_GROUP)], axis=0)

    def collapse(a):
        out = a[0:C]
        for hh in range(1, RWKV_GROUP):
            out = out + a[hh * C:(hh + 1) * C]
        return out

    groups = range(ngroups)
    sls = [slice(gi * gw, (gi + 1) * gw) for gi in groups]
    st = [state[gi] for gi in groups]
    lw = [lw_ref[:, sl] for sl in sls]
    r = [r_ref[:, sl].astype(F32) for sl in sls]
    k2 = [k2_ref[:, sl].astype(F32) for sl in sls]
    v = [v_ref[:, sl] for sl in sls]
    kk = [kk_ref[:, sl].astype(F32) for sl in sls]
    b = [b_ref[:, sl].astype(F32) for sl in sls]
    G = []
    for gi in groups:
        lhi, llo = _split(lw[gi])
        G.append(_dot(tri, lhi) + _dot(tri, llo))
    gc = [g[C - 1:C, :] for g in G]
    e_ng = [jnp.exp(-g) for g in G]
    e_c = [jnp.exp(gc[gi] - G[gi]) for gi in groups]
    rt = [(r[gi] * jnp.exp(G[gi])).astype(BF16) for gi in groups]
    kt = [(kk[gi] * jnp.exp(G[gi] - lw[gi])).astype(BF16) for gi in groups]
    kt_e = [expand(a) for a in kt]
    rt_e = [expand(a) for a in rt]
    kh_e = [expand((k2[gi] * e_ng[gi]).astype(BF16)) for gi in groups]
    bh_e = [expand((b[gi] * e_ng[gi]).astype(BF16)) for gi in groups]
    v_e = [expand(a) for a in v]
    p = [jnp.where(strict, -_dot_nt(kt_e[gi], bh_e[gi]), 0.0).astype(BF16) for gi in groups]
    m_kk = [jnp.where(strict, _dot_nt(kt_e[gi], kh_e[gi]), 0.0) for gi in groups]
    m_rk = [jnp.where(incl, _dot_nt(rt_e[gi], kh_e[gi]), 0.0) for gi in groups]
    m_rb = [jnp.where(incl, _dot_nt(rt_e[gi], bh_e[gi]), 0.0) for gi in groups]
    u = [_dot(m_kk[gi], v_e[gi]) + expand(_dot_nt(kt[gi], st[gi])) for gi in groups]
    for it in range(6):
        u = [u[gi] + _dot(p[gi], u[gi]) for gi in groups]
        if it < 5:
            p = [_dot(p[gi], p[gi]).astype(BF16) for gi in groups]
    ys = [collapse(_dot(m_rk[gi], v_e[gi]) - _dot(m_rb[gi], u[gi])) + _dot_nt(rt[gi], st[gi]) for gi in groups]
    for gi in groups:
        un = collapse(u[gi])
        upd = _dot_tn(v[gi], k2[gi] * e_c[gi]) - _dot_tn(un, b[gi] * e_c[gi])
        state[gi] = st[gi] * jnp.exp(gc[gi]) + jnp.where(same, upd, 0.0)
    y = jnp.concatenate(ys, axis=1)
    bd = bd_ref[...]
    mu = _dot_x2(y, bd) * (1.0 / n)
    yc = y - mu
    var = _dot_x2(yc * yc, bd) * (1.0 / n)
    gn = gn_ref[...]
    ygn = yc * lax.rsqrt(var + RWKV_GN_EPS) * gn[0:1] + gn[1:2]
    o_ref[...] = ((ygn + bonus_ref[...].astype(F32)) * g_ref[...].astype(F32)).astype(BF16)


def _rwkv_mixer(x2, mod_l, bsz, seq, mu, w_in, w0, w_w1, w_w2, a0, w_a1, w_a2, w_g1, w_g2, k_k, k_a, r_k,
                gn_g, gn_b):
    ntok = x2.shape[0]
    tiles = ntok // ROW_TILE
    tps = seq // ROW_TILE
    d = D_MODEL
    wr, wk, wv = (w_in[j].astype(BF16) for j in range(3))
    ww1 = _pad_to(w_w1, 1, LANES).astype(BF16)
    ww2 = _pad_to(w_w2, 0, LANES).astype(BF16)
    wa1 = _pad_to(w_a1, 1, LANES).astype(BF16)
    wa2 = _pad_to(w_a2, 0, LANES).astype(BF16)
    wg1 = _pad_to(w_g1, 1, 2 * LANES).astype(BF16)
    wg2 = _pad_to(w_g2, 0, 2 * LANES).astype(BF16)
    vec = _pad_to(jnp.stack([w0, a0, k_k, k_a, r_k.reshape(d)]).astype(F32), 0, 8)
    hid = jnp.arange(d) // RWKV_HEAD_DIM
    bd = (hid[:, None] == hid[None, :]).astype(BF16)
    outs = pl.pallas_call(
        functools.partial(_rwkv_in_kernel, tiles_per_seq=tps),
        grid=(tiles,),
        in_specs=[_row_spec(d), _mod_spec(tps), _full_spec((6, d)), _full_spec((d, d)), _full_spec((d, d)),
                  _full_spec((d, d)), _full_spec(ww1.shape), _full_spec(ww2.shape), _full_spec(wa1.shape),
                  _full_spec(wa2.shape), _full_spec(wg1.shape), _full_spec(wg2.shape), _full_spec((8, d)),
                  _full_spec((d, d))],
        out_specs=[_row_spec(d)] * 8,
        out_shape=[jax.ShapeDtypeStruct((ntok, d), F32 if j == 1 else BF16) for j in range(8)],
        scratch_shapes=[pltpu.VMEM((ROW_TILE + 8, d), F32)],
        compiler_params=_cparams("arbitrary"),
        name="rwkv_in",
    )(x2, mod_l, mu.astype(F32), wr, wk, wv, ww1, ww2, wa1, wa2, wg1, wg2, vec, bd)
    r, lw, k2, v, kk, b, g, bonus = outs

    C = RWKV_CHUNK
    nc = seq // C
    chunk = pl.BlockSpec((C, d), lambda bb, c: (bb * nc + c, 0))
    full2 = lambda shape: pl.BlockSpec(shape, lambda bb, c: (0, 0))
    gn = _pad_to(jnp.stack([gn_g, gn_b]).astype(F32), 0, 8)
    tri = jnp.tril(jnp.ones((C, C), BF16))
    gw = RWKV_GROUP * RWKV_HEAD_DIM
    return pl.pallas_call(
        _rwkv_scan_kernel,
        grid=(bsz, nc),
        in_specs=[chunk] * 8 + [full2((8, d)), full2((C, C)), full2((d, d))],
        out_specs=chunk,
        out_shape=jax.ShapeDtypeStruct((ntok, d), BF16),
        scratch_shapes=[pltpu.VMEM((d // gw, gw, gw), F32)],
        compiler_params=_cparams("parallel", "arbitrary"),
        name="rwkv_scan",
    )(r, lw, k2, v, kk, b, g, bonus, gn, tri, bd)


def _s5_in_kernel(x_ref, mod_ref, w_ref, u_ref):
    h = _modulate(x_ref[...], mod_ref[0], 0, 1)
    u_ref[...] = _dot(h, w_ref[...])


def _s5_scan_kernel(u_ref, kb_ref, ws_ref, wo_ref, lam_ref, y_ref, e_sc, xp_sc, *, bsz):
    u = u_ref[0]
    nchunk = u.shape[0] // bsz
    e_sc[...] = _dot(u, ws_ref[0, 0]) + _dot(u, ws_ref[0, 1])
    lam = lam_ref[0]
    l0 = lam[0:1]
    l1 = lam[1:2]

    def step(i, xst):
        off = pl.multiple_of(i * bsz, bsz)
        xp_sc[pl.ds(off, bsz), :] = xst
        return xst * l0 + pltpu.roll(xst, S5_STATE, axis=1) * l1 + e_sc[pl.ds(off, bsz), :]

    lax.fori_loop(0, nchunk, step, jnp.zeros((bsz, 2 * S5_STATE), F32))
    xh, xl = _split(xp_sc[...])
    y_ref[0] = (_dot(u, kb_ref[0, 0]) + _dot(u, kb_ref[0, 1])
                + _dot_hl(xh, xl, wo_ref[0, 0], wo_ref[0, 1]))


def _s5_glu_kernel(y_ref, u_ref, d_ref, w_ref, o_ref):
    y = y_ref[...] + d_ref[...] * u_ref[...]
    ge = 0.5 * y * (1.0 + jnp.tanh(math.sqrt(2.0 / math.pi) * (y + 0.044715 * (y * y * y))))
    ab = _dot(ge, w_ref[...])
    o_ref[...] = (ab[:, :D_MODEL] * _sigmoid(ab[:, D_MODEL:])).astype(BF16)


def _s5_operators(a_re, a_im, log_dt, b_re, b_im, c_re, c_im):
    hp = lax.Precision.HIGHEST
    L = S5_CHUNK
    a_re, a_im = a_re.astype(F32), a_im.astype(F32)
    dt = jnp.exp(log_dt.astype(F32))[:, None]
    mag = jnp.exp(a_re * dt)
    ab_re, ab_im = mag * jnp.cos(a_im * dt), mag * jnp.sin(a_im * dt)
    den = jnp.square(a_re) + jnp.square(a_im)
    f_re = ((ab_re - 1.0) * a_re + ab_im * a_im) / den
    f_im = (ab_im * a_re - (ab_re - 1.0) * a_im) / den
    b_re, b_im = b_re.astype(F32), b_im.astype(F32)
    bb_re = f_re[..., None] * b_re - f_im[..., None] * b_im
    bb_im = f_re[..., None] * b_im + f_im[..., None] * b_re
    tau = jnp.arange(L + 1, dtype=F32)[:, None, None]
    pmag = jnp.exp(tau * (a_re * dt)[None])
    pw_re = pmag * jnp.cos(tau * (a_im * dt)[None])
    pw_im = pmag * jnp.sin(tau * (a_im * dt)[None])
    c_re, c_im = c_re.astype(F32), c_im.astype(F32)
    cl_re = c_re[None] * pw_re[:, :, None, :] - c_im[None] * pw_im[:, :, None, :]
    cl_im = c_re[None] * pw_im[:, :, None, :] + c_im[None] * pw_re[:, :, None, :]
    kern = (jnp.einsum('tgcp,gpd->tgcd', cl_re, bb_re, precision=hp)
            - jnp.einsum('tgcp,gpd->tgcd', cl_im, bb_im, precision=hp))
    s_idx = jnp.arange(L)[:, None]
    t_idx = jnp.arange(L)[None, :]
    lag = jnp.clip(t_idx - s_idx, 0, L)
    kt = kern[lag]
    kt = jnp.where((t_idx >= s_idx)[:, :, None, None, None], kt, 0.0)
    kbig = jnp.transpose(kt, (2, 0, 4, 1, 3)).reshape(S5_N_GROUPS, L * S5_GROUP, L * S5_GROUP)
    rev_re = pw_re[L - 1 - jnp.arange(L)]
    rev_im = pw_im[L - 1 - jnp.arange(L)]
    st_re = rev_re[:, :, :, None] * bb_re[None] - rev_im[:, :, :, None] * bb_im[None]
    st_im = rev_re[:, :, :, None] * bb_im[None] + rev_im[:, :, :, None] * bb_re[None]
    wst = jnp.concatenate([jnp.transpose(st_re, (1, 0, 3, 2)), jnp.transpose(st_im, (1, 0, 3, 2))], axis=-1)
    wst = wst.reshape(S5_N_GROUPS, L * S5_GROUP, 2 * S5_STATE)
    o_re = jnp.transpose(cl_re[1:], (1, 3, 0, 2)).reshape(S5_N_GROUPS, S5_STATE, L * S5_GROUP)
    o_im = jnp.transpose(cl_im[1:], (1, 3, 0, 2)).reshape(S5_N_GROUPS, S5_STATE, L * S5_GROUP)
    wout = jnp.concatenate([o_re, -o_im], axis=1)
    lam = jnp.stack([jnp.concatenate([pw_re[L], pw_re[L]], -1), jnp.concatenate([-pw_im[L], pw_im[L]], -1)], axis=1)
    hl = lambda w: jnp.stack(_split(w), axis=1)
    return hl(kbig), hl(wst), hl(wout), lam


def _s5_mixer(x2, mod_l, bsz, seq, w_in, a_re, a_im, log_dt, b_re, b_im, c_re, c_im, d_skip, w_glu):
    ntok = x2.shape[0]
    tiles = ntok // ROW_TILE
    tps = seq // ROW_TILE
    d = D_MODEL
    u = pl.pallas_call(
        _s5_in_kernel,
        grid=(tiles,),
        in_specs=[_row_spec(d), _mod_spec(tps), _full_spec((d, d))],
        out_specs=_row_spec(d),
        out_shape=jax.ShapeDtypeStruct((ntok, d), F32),
        compiler_params=_cparams("parallel"),
        name="s5_in",
    )(x2, mod_l, w_in.astype(BF16))

    L = S5_CHUNK
    nchunk = seq // L
    rows = nchunk * bsz
    width = L * S5_GROUP
    kbig, wst, wout, lam = _s5_operators(a_re, a_im, log_dt, b_re, b_im, c_re, c_im)
    ug = jnp.transpose(u.reshape(bsz, nchunk, L, S5_N_GROUPS, S5_GROUP), (3, 1, 0, 2, 4))
    ug = ug.reshape(S5_N_GROUPS, rows, width).astype(BF16)
    grp = lambda *shape: pl.BlockSpec((1,) + shape, lambda g: (g,) + (0,) * len(shape))
    yg = pl.pallas_call(
        functools.partial(_s5_scan_kernel, bsz=bsz),
        grid=(S5_N_GROUPS,),
        in_specs=[grp(rows, width), grp(2, width, width), grp(2, width, 2 * S5_STATE),
                  grp(2, 2 * S5_STATE, width), grp(2, 2 * S5_STATE)],
        out_specs=grp(rows, width),
        out_shape=jax.ShapeDtypeStruct((S5_N_GROUPS, rows, width), F32),
        scratch_shapes=[pltpu.VMEM((rows, 2 * S5_STATE), F32), pltpu.VMEM((rows, 2 * S5_STATE), F32)],
        compiler_params=_cparams("parallel"),
        name="s5_scan",
    )(ug, kbig, wst, wout, lam)
    y = jnp.transpose(yg.reshape(S5_N_GROUPS, nchunk, bsz, L, S5_GROUP), (2, 1, 3, 0, 4)).reshape(ntok, d)

    return pl.pallas_call(
        _s5_glu_kernel,
        grid=(tiles,),
        in_specs=[_row_spec(d), _row_spec(d), _full_spec((1, d)), _full_spec((d, 2 * d))],
        out_specs=_row_spec(d),
        out_shape=jax.ShapeDtypeStruct((ntok, d), BF16),
        compiler_params=_cparams("parallel"),
        name="s5_glu",
    )(y, u, d_skip.astype(F32).reshape(1, d), w_glu.astype(BF16))


def _post_kernel(x_ref, y_ref, w_ref, mod_ref, ln_ref, rw_ref, rb_ref, tris_ref,
                 xn_ref, h2_ref, rout_ref, cnt_ref, run):
    i = pl.program_id(0)
    tm = x_ref.shape[0]

    @pl.when(i == 0)
    def _():
        run[...] = jnp.zeros(run.shape, F32)

    m = mod_ref[0]
    ln = ln_ref[...]
    y = jnp.dot(y_ref[...], w_ref[...], preferred_element_type=F32)
    xn = _layer_norm(DEEPNORM_ALPHA * x_ref[...] + (1.0 + m[2:3]) * y, ln[0:1], ln[1:2])
    xn_ref[...] = xn
    h2 = _modulate(xn, m, 3, 4)
    h2_ref[...] = h2

    hh, hl = _split(h2)
    logits = _dot_nt(rw_ref[0], hh) + _dot_nt(rw_ref[0], hl) + _dot_nt(rw_ref[1], hh)
    ex = jnp.exp(logits - jnp.max(logits, axis=0, keepdims=True))
    probs = ex / jnp.sum(ex, axis=0, keepdims=True)
    sel = probs + rb_ref[...]

    def first_max(vals):
        best = vals[0]
        for v in vals[1:]:
            best = jnp.maximum(best, v)
        idx = jnp.full(best.shape, len(vals) - 1, jnp.int32)
        for j in range(len(vals) - 2, -1, -1):
            idx = jnp.where(vals[j] == best, j, idx)
        return best, idx

    scores, firsts, seconds = [], [], []
    for g in range(N_EXPERT_GROUPS):
        s = [sel[g * EXPERTS_PER_GROUP + j:g * EXPERTS_PER_GROUP + j + 1, :] for j in range(EXPERTS_PER_GROUP)]
        v1, i1 = first_max(s)
        v2, i2 = first_max([jnp.where(i1 == j, NEG_BIG, s[j]) for j in range(EXPERTS_PER_GROUP)])
        scores.append(v1 + v2)
        firsts.append(i1)
        seconds.append(i2)
    _, grp = first_max(scores)
    e0 = jnp.zeros_like(grp)
    e1 = jnp.zeros_like(grp)
    for g in range(N_EXPERT_GROUPS):
        e0 = jnp.where(grp == g, g * EXPERTS_PER_GROUP + firsts[g], e0)
        e1 = jnp.where(grp == g, g * EXPERTS_PER_GROUP + seconds[g], e1)

    eidx = lax.broadcasted_iota(jnp.int32, (N_EXPERTS, tm), 0)
    hit0 = eidx == e0
    hit1 = eidx == e1
    w0 = jnp.sum(jnp.where(hit0, probs, 0.0), axis=0, keepdims=True)
    w1 = jnp.sum(jnp.where(hit1, probs, 0.0), axis=0, keepdims=True)
    wsum = w0 + w1
    onehot = hit0.astype(F32) + hit1.astype(F32)
    before = _dot(onehot, tris_ref[...]) + run[...]
    r0 = jnp.sum(jnp.where(hit0, before, 0.0), axis=0, keepdims=True)
    r1 = jnp.sum(jnp.where(hit1, before, 0.0), axis=0, keepdims=True)
    run[...] = run[...] + jnp.sum(onehot, axis=1, keepdims=True)
    cnt_ref[...] = jnp.broadcast_to(run[...], cnt_ref.shape)

    rid = lax.broadcasted_iota(jnp.int32, (8, tm), 0)
    rows = (e0.astype(F32), e1.astype(F32), w0 / wsum, w1 / wsum, r0, r1)
    out = jnp.zeros((8, tm), F32)
    for j, val in enumerate(rows):
        out = jnp.where(rid == j, val, out)
    rout_ref[...] = out


def _post(x2, y_in, w_out, mod_l, ln_g, ln_b, rw, rb, seq):
    ntok = x2.shape[0]
    tiles = ntok // ROW_TILE
    tps = seq // ROW_TILE
    d = D_MODEL
    kin = y_in.shape[1]
    ln = _pad_to(jnp.stack([ln_g, ln_b]).astype(F32), 0, 8)
    tris = jnp.triu(jnp.ones((ROW_TILE, ROW_TILE), BF16), 1)
    return pl.pallas_call(
        _post_kernel,
        grid=(tiles,),
        in_specs=[_row_spec(d), _row_spec(kin), _full_spec((kin, d)), _mod_spec(tps), _full_spec((8, d)),
                  _full_spec((2, N_EXPERTS, d)), _full_spec((N_EXPERTS, 1)), _full_spec((ROW_TILE, ROW_TILE))],
        out_specs=[_row_spec(d), _row_spec(d), pl.BlockSpec((8, ROW_TILE), lambda i: (0, i)),
                   _full_spec((N_EXPERTS, LANES))],
        out_shape=[jax.ShapeDtypeStruct((ntok, d), F32), jax.ShapeDtypeStruct((ntok, d), F32),
                   jax.ShapeDtypeStruct((8, ntok), F32), jax.ShapeDtypeStruct((N_EXPERTS, LANES), F32)],
        scratch_shapes=[pltpu.VMEM((N_EXPERTS, 1), F32)],
        compiler_params=_cparams("arbitrary"),
        name="post",
    )(x2, y_in, w_out.astype(BF16), mod_l, ln, rw, rb, tris)


def _dispatch_kernel(dest_ref, h_ref, zero_ref, xs_ref, sem):
    del zero_ref
    tm = h_ref.shape[0]

    def copy(t, slot):
        return pltpu.make_async_copy(h_ref.at[pl.ds(t, 1)], xs_ref.at[pl.ds(dest_ref[0, 0, slot * tm + t], 1)],
                                     sem.at[0])

    def issue(t, carry):
        copy(t, 0).start()
        copy(t, 1).start()
        return carry

    def drain(t, carry):
        copy(t, 0).wait()
        copy(t, 1).wait()
        return carry

    lax.fori_loop(0, tm, issue, 0)
    lax.fori_loop(0, tm, drain, 0)


def _ffn_kernel(be_ref, nu_ref, x_ref, wu_ref, wd_ref, o_ref, wub, wdb):
    i = pl.program_id(0)
    prev = be_ref[jnp.maximum(i - 1, 0)]
    rows = 128

    @pl.when((i == 0) | (be_ref[i] != prev))
    def _():
        def cast(j, carry):
            sl = pl.ds(pl.multiple_of(j * rows, rows), rows)
            wub[sl, :] = wu_ref[0, sl, :].astype(BF16)
            wdb[sl, :] = wd_ref[0, sl, :].astype(BF16)
            return carry
        lax.fori_loop(0, D_MODEL // rows, cast, 0)

    @pl.when(i < nu_ref[0])
    def _():
        hid = jnp.dot(x_ref[...].astype(BF16), wub[...], preferred_element_type=F32)
        act = _silu(hid[:, :D_MODEL]) * hid[:, D_MODEL:]
        o_ref[...] = jnp.dot(act.astype(BF16), wdb[...], preferred_element_type=F32)

    @pl.when(i >= nu_ref[0])
    def _():
        o_ref[...] = jnp.zeros(o_ref.shape, F32)


def _combine_kernel(dest_ref, xn_ref, wc_ref, mod_ref, ln_ref, ys_ref, o_ref, buf0, buf1, sem):
    tm = xn_ref.shape[0]
    bufs = (buf0, buf1)

    def copy(t, slot):
        return pltpu.make_async_copy(ys_ref.at[pl.ds(dest_ref[0, 0, slot * tm + t], 1)], bufs[slot].at[pl.ds(t, 1)],
                                     sem.at[0])

    def issue(t, carry):
        copy(t, 0).start()
        copy(t, 1).start()
        return carry

    def drain(t, carry):
        copy(t, 0).wait()
        copy(t, 1).wait()
        return carry

    lax.fori_loop(0, tm, issue, 0)
    lax.fori_loop(0, tm, drain, 0)
    m = mod_ref[0]
    ln = ln_ref[...]
    wc = wc_ref[...]
    y = wc[:, 0:1] * buf0[...] + wc[:, 1:2] * buf1[...]
    o_ref[...] = _layer_norm(DEEPNORM_ALPHA * xn_ref[...] + (1.0 + m[5:6]) * y, ln[0:1], ln[1:2])


def _moe(xn, h2, rout, cnt, mod_l, ln_g, ln_b, w_up, w_down, seq):
    ntok = xn.shape[0]
    d = D_MODEL
    tiles = ntok // ROW_TILE
    tps = seq // ROW_TILE
    bm = MOE_ROWS
    n_assign = ntok * TOP_K
    n_blocks = n_assign // bm + N_EXPERTS
    n_pad = n_blocks * bm

    e0 = rout[0].astype(jnp.int32)
    e1 = rout[1].astype(jnp.int32)
    counts = cnt[:, 0].astype(jnp.int32)
    padded = (counts + bm - 1) // bm * bm
    pad_end = jnp.cumsum(padded)
    pad_start = pad_end - padded
    dest0 = pad_start[e0] + rout[4].astype(jnp.int32)
    dest1 = pad_start[e1] + rout[5].astype(jnp.int32)
    dest = jnp.concatenate([dest0.reshape(tiles, 1, ROW_TILE), dest1.reshape(tiles, 1, ROW_TILE)], axis=-1)
    block_expert = jnp.minimum(jnp.searchsorted(pad_end, jnp.arange(n_blocks) * bm, side='right'),
                               N_EXPERTS - 1).astype(jnp.int32)
    n_used = (pad_end[-1] // bm).astype(jnp.int32).reshape(1)
    wcol = jnp.transpose(rout[2:4])

    dest_spec = pl.BlockSpec((1, 1, 2 * ROW_TILE), lambda i: (i, 0, 0), memory_space=pltpu.SMEM)
    xs = pl.pallas_call(
        _dispatch_kernel,
        grid=(tiles,),
        in_specs=[dest_spec, _row_spec(d), pl.BlockSpec(memory_space=pl.ANY)],
        out_specs=pl.BlockSpec(memory_space=pl.ANY),
        out_shape=jax.ShapeDtypeStruct((n_pad, d), F32),
        scratch_shapes=[pltpu.SemaphoreType.DMA((1,))],
        input_output_aliases={2: 0},
        compiler_params=_cparams("arbitrary"),
        name="moe_dispatch",
    )(dest, h2, jnp.zeros((n_pad, d), F32))

    ys = pl.pallas_call(
        _ffn_kernel,
        grid_spec=pltpu.PrefetchScalarGridSpec(
            num_scalar_prefetch=2,
            grid=(n_blocks,),
            in_specs=[pl.BlockSpec((bm, d), lambda i, be, nu: (i, 0)),
                      pl.BlockSpec((1, d, 2 * d), lambda i, be, nu: (be[i], 0, 0)),
                      pl.BlockSpec((1, d, d), lambda i, be, nu: (be[i], 0, 0))],
            out_specs=pl.BlockSpec((bm, d), lambda i, be, nu: (i, 0)),
            scratch_shapes=[pltpu.VMEM((d, 2 * d), BF16), pltpu.VMEM((d, d), BF16)],
        ),
        out_shape=jax.ShapeDtypeStruct((n_pad, d), F32),
        compiler_params=_cparams("arbitrary"),
        name="moe_ffn",
    )(block_expert, n_used, xs, w_up, w_down)

    ln = _pad_to(jnp.stack([ln_g, ln_b]).astype(F32), 0, 8)
    return pl.pallas_call(
        _combine_kernel,
        grid=(tiles,),
        in_specs=[dest_spec, _row_spec(d), pl.BlockSpec((ROW_TILE, 2), lambda i: (i, 0)), _mod_spec(tps),
                  _full_spec((8, d)), pl.BlockSpec(memory_space=pl.ANY)],
        out_specs=_row_spec(d),
        out_shape=jax.ShapeDtypeStruct((ntok, d), F32),
        scratch_shapes=[pltpu.VMEM((ROW_TILE, d), F32), pltpu.VMEM((ROW_TILE, d), F32),
                        pltpu.SemaphoreType.DMA((1,))],
        compiler_params=_cparams("arbitrary"),
        name="moe_combine",
    )(dest, xn, wcol, mod_l, ln, ys)


def kernel(x, c, ada_w, ada_b, ln1_g, ln1_b, ln2_g, ln2_b, ssd_w_in, ssd_conv_w, ssd_conv_b, ssd_dt_bias, ssd_a_log, ssd_d, ssd_norm_g, ssd_w_out, gla_w_in, gla_w_gate, gla_b_gate, gla_norm_g, gla_w_out, rwkv_mu, rwkv_w_in, rwkv_w0, rwkv_w_w1, rwkv_w_w2, rwkv_a0, rwkv_w_a1, rwkv_w_a2, rwkv_w_g1, rwkv_w_g2, rwkv_k_k, rwkv_k_a, rwkv_r_k, rwkv_gn_g, rwkv_gn_b, rwkv_w_out, s5_w_in, s5_a_re, s5_a_im, s5_log_dt, s5_b_re, s5_b_im, s5_c_re, s5_c_im, s5_d, s5_w_glu, s5_w_out, moe_w_up, moe_w_down, router_w, router_b):
    bsz, seq, d = x.shape
    assert d == D_MODEL and seq % ROW_TILE == 0 and seq % SSD_CHUNK == 0
    depth = ada_w.shape[0]
    mod = _adaln_mod(c, ada_w, ada_b).reshape(depth, bsz, 6, d)
    rw = _hilo(jnp.transpose(router_w))
    rb = router_b.astype(F32).reshape(N_EXPERTS, 1)
    x2 = x.reshape(bsz * seq, d)
    for i in range(depth):
        kind, j = i % 4, i // 4
        mod_l = mod[i]
        if kind == 0:
            y = _ssd_mixer(x2, mod_l, bsz, seq, ssd_w_in[j], ssd_conv_w[j], ssd_conv_b[j], ssd_dt_bias[j],
                           ssd_a_log[j], ssd_d[j], ssd_norm_g[j])
            w_out = ssd_w_out[j]
        elif kind == 1:
            y = _gla_mixer(x2, mod_l, bsz, seq, gla_w_in[j], gla_w_gate[j], gla_b_gate[j], gla_norm_g[j])
            w_out = gla_w_out[j]
        elif kind == 2:
            y = _rwkv_mixer(x2, mod_l, bsz, seq, rwkv_mu[j], rwkv_w_in[j], rwkv_w0[j], rwkv_w_w1[j], rwkv_w_w2[j],
                            rwkv_a0[j], rwkv_w_a1[j], rwkv_w_a2[j], rwkv_w_g1[j], rwkv_w_g2[j], rwkv_k_k[j],
                            rwkv_k_a[j], rwkv_r_k[j], rwkv_gn_g[j], rwkv_gn_b[j])
            w_out = rwkv_w_out[j]
        else:
            y = _s5_mixer(x2, mod_l, bsz, seq, s5_w_in[j], s5_a_re[j], s5_a_im[j], s5_log_dt[j], s5_b_re[j],
                          s5_b_im[j], s5_c_re[j], s5_c_im[j], s5_d[j], s5_w_glu[j])
            w_out = s5_w_out[j]
        xn, h2, rout, cnt = _post(x2, y, w_out, mod_l, ln1_g[i], ln1_b[i], rw, rb, seq)
        x2 = _moe(xn, h2, rout, cnt, mod_l, ln2_g[i], ln2_b[i], moe_w_up[i], moe_w_down[i], seq)
    return x2.reshape(bsz, seq, d)
```

```python
import functools
import math

import jax
import jax.numpy as jnp
from jax import lax
from jax.experimental import pallas as pl
from jax.experimental.pallas import tpu as pltpu

F32 = jnp.float32
BF16 = jnp.bfloat16

D_MODEL = 1024
DEPTH = 4
DEEPNORM_ALPHA = (2 * DEPTH) ** 0.25
LN_EPS = 1e-5
RMS_EPS = 1e-5

SSD_D_INNER = 2048
SSD_HEAD_DIM = 64
SSD_N_HEADS = 32
SSD_N_GROUPS = 4
SSD_D_STATE = 128
SSD_CONV = 4
SSD_CONV_DIM = SSD_D_INNER + 2 * SSD_N_GROUPS * SSD_D_STATE
SSD_CHUNK = 128

GLA_N_HEADS = 4
GLA_D_K = 512
GLA_D_V = 1024
GLA_GATE_RANK = 16
GLA_TAU = 16.0
GLA_SUB = 32
GLA_ROWS = 256

RWKV_HEAD_DIM = 64
RWKV_N_HEADS = 16
RWKV_GN_EPS = 64e-5
RWKV_CHUNK = 64
RWKV_GROUP = 4

S5_GROUP = 16
S5_N_GROUPS = 64
S5_STATE = 64
S5_CHUNK = 16
S5_PACK = 8
S5_STEP_CHUNKS = 16

N_EXPERTS = 16
N_EXPERT_GROUPS = 4
EXPERTS_PER_GROUP = 4
TOP_K = 2
MOE_ROWS = 256

ROW_TILE = 256
LANES = 128
VMEM_LIMIT = 56 * 1024 * 1024
NEG_BIG = -1e30


def _cparams(*sem):
    return pltpu.CompilerParams(dimension_semantics=sem, vmem_limit_bytes=VMEM_LIMIT)


def _dot(a, b):
    return jnp.dot(a.astype(BF16), b.astype(BF16), preferred_element_type=F32)


def _dot_nt(a, b):
    return lax.dot_general(a.astype(BF16), b.astype(BF16), (((1,), (1,)), ((), ())), preferred_element_type=F32)


def _dot_tn(a, b):
    return lax.dot_general(a.astype(BF16), b.astype(BF16), (((0,), (0,)), ((), ())), preferred_element_type=F32)


def _split(x):
    hi = x.astype(BF16)
    lo = (x - hi.astype(F32)).astype(BF16)
    return hi, lo


def _split3(x):
    h1 = x.astype(BF16)
    r1 = x - h1.astype(F32)
    h2 = r1.astype(BF16)
    h3 = (r1 - h2.astype(F32)).astype(BF16)
    return h1, h2, h3


def _dot_x2(a, b_exact):
    hi, lo = _split(a)
    return _dot(hi, b_exact) + _dot(lo, b_exact)


def _dot_x3(a, b_exact):
    h1, h2, h3 = _split3(a)
    return _dot(h1, b_exact) + _dot(h2, b_exact) + _dot(h3, b_exact)


def _dot_hl(a_hi, a_lo, b_hi, b_lo):
    return _dot(a_hi, b_hi) + _dot(a_lo, b_hi) + _dot(a_hi, b_lo)


def _sigmoid(x):
    return 1.0 / (1.0 + jnp.exp(-x))


def _silu(x):
    return x * _sigmoid(x)


def _softplus(x):
    return jnp.maximum(x, 0.0) + jnp.log1p(jnp.exp(-jnp.abs(x)))


def _hilo(w):
    hi, lo = _split(w.astype(F32))
    return jnp.stack([hi, lo])


def _pad_to(a, axis, size):
    pad = [(0, 0)] * a.ndim
    pad[axis] = (0, size - a.shape[axis])
    return jnp.pad(a, pad)


def _modulate(x, m, shift_row, scale_row):
    return x * (1.0 + m[scale_row:scale_row + 1]) + m[shift_row:shift_row + 1]


def _layer_norm(v, g, b):
    mu = jnp.mean(v, axis=-1, keepdims=True)
    vc = v - mu
    var = jnp.mean(vc * vc, axis=-1, keepdims=True)
    return vc * lax.rsqrt(var + LN_EPS) * g + b


def _mod_kernel(c_ref, w_ref, b_ref, o_ref):
    o_ref[0] = _dot(_silu(c_ref[...]), w_ref[0]) + b_ref[0]


def _adaln_mod(c, ada_w, ada_b):
    nl, d, n6 = ada_w.shape
    bsz = c.shape[0]
    tn = 512
    return pl.pallas_call(
        _mod_kernel,
        grid=(nl, n6 // tn),
        in_specs=[
            pl.BlockSpec((bsz, d), lambda l, j: (0, 0)),
            pl.BlockSpec((1, d, tn), lambda l, j: (l, 0, j)),
            pl.BlockSpec((1, 1, tn), lambda l, j: (l, 0, j)),
        ],
        out_specs=pl.BlockSpec((1, bsz, tn), lambda l, j: (l, 0, j)),
        out_shape=jax.ShapeDtypeStruct((nl, bsz, n6), F32),
        compiler_params=_cparams("parallel", "parallel"),
        name="adaln_mod",
    )(c, ada_w, ada_b.reshape(nl, 1, n6))


def _row_spec(width, tm=ROW_TILE):
    return pl.BlockSpec((tm, width), lambda i: (i, 0))


def _full_spec(shape):
    nd = len(shape)
    return pl.BlockSpec(shape, lambda *_: (0,) * nd)


def _mod_spec(tiles_per_seq):
    return pl.BlockSpec((1, 6, D_MODEL), lambda i: (i // tiles_per_seq, 0, 0))


def _ssd_in_kernel(x_ref, mod_ref, wz_ref, wx_ref, wdt_ref, z_ref, xbc_ref, dt_ref):
    h = _modulate(x_ref[...], mod_ref[0], 0, 1)
    hb, hl = _split(h)
    z_ref[...] = jnp.dot(hb, wz_ref[...], preferred_element_type=F32).astype(BF16)
    xbc_ref[...] = jnp.dot(hb, wx_ref[...], preferred_element_type=F32)
    dt_ref[...] = _dot_hl(hb, hl, wdt_ref[0], wdt_ref[1])


def _ssd_scan_kernel(z_ref, xbc_ref, dt_ref, cw_ref, cb_ref, dtb_ref, an_ref, dsk_ref, ng_ref,
                     e_ref, tri_ref, triu_ref, o_ref, xpad, state):
    c = pl.program_id(1)
    L = z_ref.shape[0]
    nstate = SSD_D_STATE

    @pl.when(c == 0)
    def _():
        state[...] = jnp.zeros(state.shape, F32)
        xpad[0:8, :] = jnp.zeros((8, SSD_CONV_DIM), F32)

    @pl.when(c != 0)
    def _():
        xpad[0:8, :] = xpad[L:L + 8, :]

    xpad[8:8 + L, :] = xbc_ref[...]

    acc = cb_ref[...] + cw_ref[3:4, :] * xpad[8:8 + L, :]
    for j in range(1, SSD_CONV):
        acc = acc + cw_ref[3 - j:4 - j, :] * xpad[pl.ds(8 - j, L), :]
    act = _silu(acc)
    xs = act[:, :SSD_D_INNER]
    bm = act[:, SSD_D_INNER:SSD_D_INNER + SSD_N_GROUPS * nstate]
    cm = act[:, SSD_D_INNER + SSD_N_GROUPS * nstate:]

    dtv = _softplus(dt_ref[...] + dtb_ref[...])
    dta = dtv * an_ref[...]
    dhi, dlo = _split(dta)
    tri = tri_ref[...]
    triu = triu_ref[...]
    acum = _dot(tri, dhi) + _dot(tri, dlo)
    acum_t = _dot_tn(dhi, triu) + _dot_tn(dlo, triu)

    e = e_ref[...]
    acum_x = _dot_x3(acum, e)
    dt_x = _dot_x3(dtv, e)
    eac_x = jnp.exp(acum_x)
    last_x = acum_x[L - 1:L, :]
    elast_x = jnp.exp(last_x)
    xdt = xs * dt_x
    xw = xs * (jnp.exp(last_x - acum_x) * dt_x)

    row = lax.broadcasted_iota(jnp.int32, (L, L), 0)
    col = lax.broadcasted_iota(jnp.int32, (L, L), 1)
    causal = row >= col
    lane = lax.broadcasted_iota(jnp.int32, (L, LANES), 1)
    left = lane < SSD_HEAD_DIM

    pieces = []
    heads_per_group = SSD_N_HEADS // SSD_N_GROUPS
    for g in range(SSD_N_GROUPS):
        bg = bm[:, g * nstate:(g + 1) * nstate]
        cg = cm[:, g * nstate:(g + 1) * nstate]
        cb = _dot_nt(cg, bg)
        for p in range(heads_per_group // 2):
            h0 = g * heads_per_group + 2 * p
            pair = h0 // 2
            sl = slice(h0 * SSD_HEAD_DIM, h0 * SSD_HEAD_DIM + LANES)
            ms = []
            for hh in (h0, h0 + 1):
                seg = acum[:, hh:hh + 1] - acum_t[hh:hh + 1, :]
                ms.append((cb * jnp.exp(jnp.where(causal, seg, NEG_BIG))).astype(BF16))
            mcat = jnp.concatenate(ms, axis=1)
            xp = xdt[:, sl]
            xbd = jnp.concatenate([jnp.where(left, xp, 0.0), jnp.where(left, 0.0, xp)], axis=0)
            st = state[pair]
            y_pair = _dot(mcat, xbd) + _dot(cg, st) * eac_x[:, sl]
            state[pair] = st * elast_x[:, sl] + _dot_tn(bg, xw[:, sl])
            pieces.append(y_pair)
    y = jnp.concatenate(pieces, axis=1) + dsk_ref[...] * xs
    y = y * _silu(z_ref[...].astype(F32))
    gw = SSD_D_INNER // SSD_N_GROUPS
    outs = []
    for g in range(SSD_N_GROUPS):
        yg = y[:, g * gw:(g + 1) * gw]
        outs.append(yg * lax.rsqrt(jnp.mean(yg * yg, axis=-1, keepdims=True) + RMS_EPS))
    o_ref[...] = (jnp.concatenate(outs, axis=1) * ng_ref[...]).astype(BF16)


def _ssd_mixer(x2, mod_l, bsz, seq, w_in, conv_w, conv_b, dt_bias, a_log, d_skip, norm_g):
    ntok = x2.shape[0]
    tiles = ntok // ROW_TILE
    tps = seq // ROW_TILE
    wz = w_in[:, :SSD_D_INNER].astype(BF16)
    wx = w_in[:, SSD_D_INNER:SSD_D_INNER + SSD_CONV_DIM].astype(BF16)
    wdt = _hilo(_pad_to(w_in[:, SSD_D_INNER + SSD_CONV_DIM:], 1, LANES))
    z, xbc, dt = pl.pallas_call(
        _ssd_in_kernel,
        grid=(tiles,),
        in_specs=[_row_spec(D_MODEL), _mod_spec(tps), _full_spec(wz.shape), _full_spec(wx.shape),
                  _full_spec(wdt.shape)],
        out_specs=[_row_spec(SSD_D_INNER), _row_spec(SSD_CONV_DIM), _row_spec(LANES)],
        out_shape=[jax.ShapeDtypeStruct((ntok, SSD_D_INNER), BF16),
                   jax.ShapeDtypeStruct((ntok, SSD_CONV_DIM), F32),
                   jax.ShapeDtypeStruct((ntok, LANES), F32)],
        compiler_params=_cparams("parallel"),
        name="ssd_in",
    )(x2, mod_l, wz, wx, wdt)

    L = SSD_CHUNK
    nc = seq // L
    a_neg = _pad_to((-jnp.exp(a_log.astype(F32))).reshape(1, SSD_N_HEADS), 1, LANES)
    dtb = _pad_to(dt_bias.astype(F32).reshape(1, SSD_N_HEADS), 1, LANES)
    dsk = jnp.repeat(d_skip.astype(F32), SSD_HEAD_DIM).reshape(1, SSD_D_INNER)
    expand = (jnp.arange(LANES)[:, None] == (jnp.arange(SSD_D_INNER) // SSD_HEAD_DIM)[None, :]).astype(BF16)
    tri = jnp.tril(jnp.ones((L, L), BF16))
    triu = jnp.triu(jnp.ones((L, L), BF16))
    chunk = lambda w: pl.BlockSpec((L, w), lambda b, c: (b * nc + c, 0))
    full2 = lambda shape: pl.BlockSpec(shape, lambda b, c: (0, 0))
    return pl.pallas_call(
        _ssd_scan_kernel,
        grid=(bsz, nc),
        in_specs=[chunk(SSD_D_INNER), chunk(SSD_CONV_DIM), chunk(LANES),
                  full2((SSD_CONV, SSD_CONV_DIM)), full2((1, SSD_CONV_DIM)), full2((1, LANES)), full2((1, LANES)),
                  full2((1, SSD_D_INNER)), full2((1, SSD_D_INNER)), full2((LANES, SSD_D_INNER)),
                  full2((L, L)), full2((L, L))],
        out_specs=chunk(SSD_D_INNER),
        out_shape=jax.ShapeDtypeStruct((ntok, SSD_D_INNER), BF16),
        scratch_shapes=[pltpu.VMEM((L + 8, SSD_CONV_DIM), F32),
                        pltpu.VMEM((SSD_N_HEADS // 2, SSD_D_STATE, LANES), F32)],
        compiler_params=_cparams("parallel", "arbitrary"),
        name="ssd_scan",
    )(z, xbc, dt, conv_w.astype(F32), conv_b.astype(F32).reshape(1, -1), dtb, a_neg, dsk,
      norm_g.astype(F32).reshape(1, -1), expand, tri, triu)


def _gla_in_kernel(x_ref, mod_ref, wq_ref, wk_ref, wv_ref, wr_ref, wg_ref, wgate_ref, bgate_ref,
                   q_ref, k_ref, v_ref, r_ref, la_ref):
    h = _modulate(x_ref[...], mod_ref[0], 0, 1)
    hb, hl = _split(h)
    dk = GLA_D_K // GLA_N_HEADS
    q_ref[...] = jnp.dot(hb, wq_ref[...], preferred_element_type=F32) * (dk ** -0.5)
    k_ref[...] = jnp.dot(hb, wk_ref[...], preferred_element_type=F32)
    v_ref[...] = jnp.dot(hb, wv_ref[...], preferred_element_type=F32)
    r_ref[...] = jnp.dot(hb, wr_ref[...], preferred_element_type=F32).astype(BF16)
    g_low = _dot_hl(hb, hl, wg_ref[0], wg_ref[1])
    gh, gl = _split(g_low)
    gz = _dot_hl(gh, gl, wgate_ref[0], wgate_ref[1]) + bgate_ref[...]
    la_ref[...] = (jnp.minimum(gz, 0.0) - jnp.log1p(jnp.exp(-jnp.abs(gz)))) * (1.0 / GLA_TAU)


def _gla_scan_kernel(q_ref, k_ref, v_ref, r_ref, la_ref, ng_ref, tri_ref, o_ref, state):
    c = pl.program_id(1)
    rows = q_ref.shape[0]
    sub = GLA_SUB
    nsub = rows // sub
    dk = GLA_D_K // GLA_N_HEADS
    dv = GLA_D_V // GLA_N_HEADS

    @pl.when(c == 0)
    def _():
        state[...] = jnp.zeros(state.shape, F32)

    q = q_ref[...]
    k = k_ref[...]
    v = v_ref[...]
    lhi, llo = _split(la_ref[...])
    tri = tri_ref[...]
    bcum = _dot(tri, lhi) + _dot(tri, llo)

    def sub_row(a, j):
        a3 = a.reshape(nsub, sub, a.shape[-1])
        return jnp.broadcast_to(a3[:, j:j + 1, :], a3.shape).reshape(a.shape)

    last = sub_row(bcum, sub - 1)
    qdec = q * jnp.exp(bcum)
    kdec = k * jnp.exp(last - bcum)
    elast = jnp.exp(last)

    lane = lax.broadcasted_iota(jnp.int32, (rows, LANES), 1)
    local_row = lax.broadcasted_iota(jnp.int32, (rows, LANES), 0) & (sub - 1)
    att = jnp.zeros((rows, LANES), F32)
    for h in range(GLA_N_HEADS):
        ks = slice(h * dk, (h + 1) * dk)
        qh, kh, bh = q[:, ks], k[:, ks], bcum[:, ks]
        for j in range(sub):
            ex = jnp.exp(jnp.minimum(bh - sub_row(bh, j), 0.0))
            a = jnp.sum(qh * sub_row(kh, j) * ex, axis=-1, keepdims=True)
            att = jnp.where(lane == h * sub + j, a, att)
    att = jnp.where((lane & (sub - 1)) <= local_row, att, 0.0)
    ahi, alo = _split(att)

    ng = ng_ref[...]
    value_head = lax.broadcasted_iota(jnp.int32, (sub, GLA_D_V), 1) >> (dv.bit_length() - 1)
    for s in range(nsub):
        rs = slice(s * sub, (s + 1) * sub)
        vs = v[rs]
        vbd = jnp.concatenate([jnp.where(value_head == h, vs, 0.0) for h in range(GLA_N_HEADS)], axis=0)
        o_intra = _dot(ahi[rs], vbd) + _dot(alo[rs], vbd)
        outs = []
        for h in range(GLA_N_HEADS):
            ks = slice(h * dk, (h + 1) * dk)
            st = state[h]
            oh = o_intra[:, h * dv:(h + 1) * dv] + _dot_nt(qdec[rs, ks], st)
            state[h] = st * elast[s * sub:s * sub + 1, ks] + _dot_tn(vs[:, h * dv:(h + 1) * dv], kdec[rs, ks])
            outs.append(oh * lax.rsqrt(jnp.mean(oh * oh, axis=-1, keepdims=True) + RMS_EPS))
        on = jnp.concatenate(outs, axis=1) * ng
        o_ref[rs, :] = (on * _silu(r_ref[rs, :].astype(F32))).astype(BF16)


def _gla_mixer(x2, mod_l, bsz, seq, w_in, w_gate, b_gate, norm_g):
    ntok = x2.shape[0]
    tiles = ntok // ROW_TILE
    tps = seq // ROW_TILE
    o1, o2, o3 = GLA_D_K, 2 * GLA_D_K, 2 * GLA_D_K + GLA_D_V
    o4 = o3 + GLA_D_V
    wq, wk = w_in[:, :o1].astype(BF16), w_in[:, o1:o2].astype(BF16)
    wv, wr = w_in[:, o2:o3].astype(BF16), w_in[:, o3:o4].astype(BF16)
    wg = _hilo(_pad_to(w_in[:, o4:], 1, LANES))
    wgate = _hilo(_pad_to(w_gate, 0, LANES))
    q, k, v, r, la = pl.pallas_call(
        _gla_in_kernel,
        grid=(tiles,),
        in_specs=[_row_spec(D_MODEL), _mod_spec(tps), _full_spec(wq.shape), _full_spec(wk.shape),
                  _full_spec(wv.shape), _full_spec(wr.shape), _full_spec(wg.shape), _full_spec(wgate.shape),
                  _full_spec((1, GLA_D_K))],
        out_specs=[_row_spec(GLA_D_K), _row_spec(GLA_D_K), _row_spec(GLA_D_V), _row_spec(GLA_D_V),
                   _row_spec(GLA_D_K)],
        out_shape=[jax.ShapeDtypeStruct((ntok, GLA_D_K), F32), jax.ShapeDtypeStruct((ntok, GLA_D_K), F32),
                   jax.ShapeDtypeStruct((ntok, GLA_D_V), F32), jax.ShapeDtypeStruct((ntok, GLA_D_V), BF16),
                   jax.ShapeDtypeStruct((ntok, GLA_D_K), F32)],
        compiler_params=_cparams("parallel"),
        name="gla_in",
    )(x2, mod_l, wq, wk, wv, wr, wg, wgate, b_gate.astype(F32).reshape(1, -1))

    rows = GLA_ROWS
    nc = seq // rows
    chunk = lambda w: pl.BlockSpec((rows, w), lambda b, c: (b * nc + c, 0))
    full2 = lambda shape: pl.BlockSpec(shape, lambda b, c: (0, 0))
    idx = jnp.arange(rows)
    tri = ((idx[:, None] >= idx[None, :]) & (idx[:, None] // GLA_SUB == idx[None, :] // GLA_SUB)).astype(BF16)
    return pl.pallas_call(
        _gla_scan_kernel,
        grid=(bsz, nc),
        in_specs=[chunk(GLA_D_K), chunk(GLA_D_K), chunk(GLA_D_V), chunk(GLA_D_V), chunk(GLA_D_K),
                  full2((1, GLA_D_V)), full2((rows, rows))],
        out_specs=chunk(GLA_D_V),
        out_shape=jax.ShapeDtypeStruct((ntok, GLA_D_V), BF16),
        scratch_shapes=[pltpu.VMEM((GLA_N_HEADS, GLA_D_V // GLA_N_HEADS, GLA_D_K // GLA_N_HEADS), F32)],
        compiler_params=_cparams("parallel", "arbitrary"),
        name="gla_scan",
    )(q, k, v, r, la, norm_g.astype(F32).reshape(1, -1), tri)


def _head_sum(x, sel, sel_t):
    return _dot_x3(_dot_x2(x, sel), sel_t)


def _rwkv_in_kernel(x_ref, mod_ref, mu_ref, wr_ref, wk_ref, wv_ref, ww1_ref, ww2_ref, wa1_ref, wa2_ref,
                    wg1_ref, wg2_ref, vec_ref, sel_ref, selt_ref,
                    r_ref, lw_ref, k2_ref, v_ref, kk_ref, b_ref, g_ref, bonus_ref, hp, *, tiles_per_seq):
    i = pl.program_id(0)
    tm = x_ref.shape[0]
    h = _modulate(x_ref[...], mod_ref[0], 0, 1)

    @pl.when(i % tiles_per_seq == 0)
    def _():
        hp[0:8, :] = jnp.zeros((8, D_MODEL), F32)

    @pl.when(i % tiles_per_seq != 0)
    def _():
        hp[0:8, :] = hp[tm:tm + 8, :]

    hp[8:8 + tm, :] = h
    xx = hp[pl.ds(7, tm), :] - h
    mu = mu_ref[...]
    xr, xw, xk, xv, xa, xg = (h + xx * mu[j:j + 1] for j in range(6))
    vec = vec_ref[...]
    w0, a0, k_k, k_a, r_k = (vec[j:j + 1] for j in range(5))
    sel = sel_ref[...]
    sel_t = selt_ref[...]

    r = _dot(xr, wr_ref[...])
    k = _dot(xk, wk_ref[...])
    v = _dot(xv, wv_ref[...])
    wlog = -_softplus(-(w0 + _dot(jnp.tanh(_dot(xw, ww1_ref[...])), ww2_ref[...]))) - 0.5
    a = _sigmoid(a0 + _dot(_dot(xa, wa1_ref[...]), wa2_ref[...]))
    g = _dot(_sigmoid(_dot(xg, wg1_ref[...])), wg2_ref[...])
    kx = k * k_k
    norm = jnp.sqrt(_head_sum(kx * kx, sel, sel_t))
    kk = kx / jnp.maximum(norm, 1e-12)
    k2 = k * (1.0 + (a - 1.0) * k_a)
    bonus = _head_sum(r * k2 * r_k, sel, sel_t) * v

    r_ref[...] = r.astype(BF16)
    lw_ref[...] = -jnp.exp(wlog)
    k2_ref[...] = k2.astype(BF16)
    v_ref[...] = v.astype(BF16)
    kk_ref[...] = kk.astype(BF16)
    b_ref[...] = (kk * a).astype(BF16)
    g_ref[...] = g.astype(BF16)
    bonus_ref[...] = bonus.astype(BF16)


def _rwkv_scan_kernel(r_ref, lw_ref, k2_ref, v_ref, kk_ref, b_ref, g_ref, bonus_ref, gn_ref, tri_ref, sel_ref,
                      selt_ref, o_ref, state):
    c = pl.program_id(1)
    C = RWKV_CHUNK
    n = RWKV_HEAD_DIM
    gw = RWKV_GROUP * n
    ngroups = D_MODEL // gw

    @pl.when(c == 0)
    def _():
        state[...] = jnp.zeros(state.shape, F32)

    tri = tri_ref[...]
    ri = lax.broadcasted_iota(jnp.int32, (gw, gw), 0)
    ci = lax.broadcasted_iota(jnp.int32, (gw, gw), 1)
    shift = n.bit_length() - 1
    same = (ri >> shift) == (ci >> shift)
    strict = same & ((ri & (C - 1)) > (ci & (C - 1)))
    incl = same & ((ri & (C - 1)) >= (ci & (C - 1)))
    lane_head = lax.broadcasted_iota(jnp.int32, (C, gw), 1) >> shift

    def expand(a):
        return jnp.concatenate([jnp.where(lane_head == hh, a, 0.0) for hh in range(RWKV_GROUP)], axis=0)

    def collapse(a):
        out = a[0:C]
        for hh in range(1, RWKV_GROUP):
            out = out + a[hh * C:(hh + 1) * C]
        return out

    groups = range(ngroups)
    sls = [slice(gi * gw, (gi + 1) * gw) for gi in groups]
    st = [state[gi] for gi in groups]
    lw = [lw_ref[:, sl] for sl in sls]
    r = [r_ref[:, sl].astype(F32) for sl in sls]
    k2 = [k2_ref[:, sl].astype(F32) for sl in sls]
    v = [v_ref[:, sl] for sl in sls]
    kk = [kk_ref[:, sl].astype(F32) for sl in sls]
    b = [b_ref[:, sl].astype(F32) for sl in sls]
    G = []
    for gi in groups:
        lhi, llo = _split(lw[gi])
        G.append(_dot(tri, lhi) + _dot(tri, llo))
    gc = [g[C - 1:C, :] for g in G]
    e_ng = [jnp.exp(-g) for g in G]
    e_c = [jnp.exp(gc[gi] - G[gi]) for gi in groups]
    rt = [(r[gi] * jnp.exp(G[gi])).astype(BF16) for gi in groups]
    kt = [(kk[gi] * jnp.exp(G[gi] - lw[gi])).astype(BF16) for gi in groups]
    kt_e = [expand(a) for a in kt]
    rt_e = [expand(a) for a in rt]
    kh_e = [expand((k2[gi] * e_ng[gi]).astype(BF16)) for gi in groups]
    bh_e = [expand((b[gi] * e_ng[gi]).astype(BF16)) for gi in groups]
    v_e = [expand(a) for a in v]
    p = [jnp.where(strict, -_dot_nt(kt_e[gi], bh_e[gi]), 0.0).astype(BF16) for gi in groups]
    m_kk = [jnp.where(strict, _dot_nt(kt_e[gi], kh_e[gi]), 0.0) for gi in groups]
    m_rk = [jnp.where(incl, _dot_nt(rt_e[gi], kh_e[gi]), 0.0) for gi in groups]
    m_rb = [jnp.where(incl, _dot_nt(rt_e[gi], bh_e[gi]), 0.0) for gi in groups]
    u = [_dot(m_kk[gi], v_e[gi]) + expand(_dot_nt(kt[gi], st[gi])) for gi in groups]
    for it in range(6):
        u = [u[gi] + _dot(p[gi], u[gi]) for gi in groups]
        if it < 5:
            p = [_dot(p[gi], p[gi]).astype(BF16) for gi in groups]
    ys = [collapse(_dot(m_rk[gi], v_e[gi]) - _dot(m_rb[gi], u[gi])) + _dot_nt(rt[gi], st[gi]) for gi in groups]
    for gi in groups:
        un = collapse(u[gi])
        upd = _dot_tn(v[gi], k2[gi] * e_c[gi]) - _dot_tn(un, b[gi] * e_c[gi])
        state[gi] = st[gi] * jnp.exp(gc[gi]) + jnp.where(same, upd, 0.0)
    y = jnp.concatenate(ys, axis=1)
    sel = sel_ref[...]
    sel_t = selt_ref[...]
    mu = _head_sum(y, sel, sel_t) * (1.0 / n)
    yc = y - mu
    var = _head_sum(yc * yc, sel, sel_t) * (1.0 / n)
    gn = gn_ref[...]
    ygn = yc * lax.rsqrt(var + RWKV_GN_EPS) * gn[0:1] + gn[1:2]
    o_ref[...] = ((ygn + bonus_ref[...].astype(F32)) * g_ref[...].astype(F32)).astype(BF16)


def _rwkv_mixer(x2, mod_l, bsz, seq, mu, w_in, w0, w_w1, w_w2, a0, w_a1, w_a2, w_g1, w_g2, k_k, k_a, r_k,
                gn_g, gn_b):
    ntok = x2.shape[0]
    tiles = ntok // ROW_TILE
    tps = seq // ROW_TILE
    d = D_MODEL
    wr, wk, wv = (w_in[j].astype(BF16) for j in range(3))
    ww1 = _pad_to(w_w1, 1, LANES).astype(BF16)
    ww2 = _pad_to(w_w2, 0, LANES).astype(BF16)
    wa1 = _pad_to(w_a1, 1, LANES).astype(BF16)
    wa2 = _pad_to(w_a2, 0, LANES).astype(BF16)
    wg1 = _pad_to(w_g1, 1, 2 * LANES).astype(BF16)
    wg2 = _pad_to(w_g2, 0, 2 * LANES).astype(BF16)
    vec = _pad_to(jnp.stack([w0, a0, k_k, k_a, r_k.reshape(d)]).astype(F32), 0, 8)
    sel = ((jnp.arange(d) // RWKV_HEAD_DIM)[:, None] == jnp.arange(LANES)[None, :]).astype(BF16)
    sel_t = jnp.transpose(sel)
    outs = pl.pallas_call(
        functools.partial(_rwkv_in_kernel, tiles_per_seq=tps),
        grid=(tiles,),
        in_specs=[_row_spec(d), _mod_spec(tps), _full_spec((6, d)), _full_spec((d, d)), _full_spec((d, d)),
                  _full_spec((d, d)), _full_spec(ww1.shape), _full_spec(ww2.shape), _full_spec(wa1.shape),
                  _full_spec(wa2.shape), _full_spec(wg1.shape), _full_spec(wg2.shape), _full_spec((8, d)),
                  _full_spec((d, LANES)), _full_spec((LANES, d))],
        out_specs=[_row_spec(d)] * 8,
        out_shape=[jax.ShapeDtypeStruct((ntok, d), F32 if j == 1 else BF16) for j in range(8)],
        scratch_shapes=[pltpu.VMEM((ROW_TILE + 8, d), F32)],
        compiler_params=_cparams("arbitrary"),
        name="rwkv_in",
    )(x2, mod_l, mu.astype(F32), wr, wk, wv, ww1, ww2, wa1, wa2, wg1, wg2, vec, sel, sel_t)
    r, lw, k2, v, kk, b, g, bonus = outs

    C = RWKV_CHUNK
    nc = seq // C
    chunk = pl.BlockSpec((C, d), lambda bb, c: (bb * nc + c, 0))
    full2 = lambda shape: pl.BlockSpec(shape, lambda bb, c: (0, 0))
    gn = _pad_to(jnp.stack([gn_g, gn_b]).astype(F32), 0, 8)
    tri = jnp.tril(jnp.ones((C, C), BF16))
    gw = RWKV_GROUP * RWKV_HEAD_DIM
    return pl.pallas_call(
        _rwkv_scan_kernel,
        grid=(bsz, nc),
        in_specs=[chunk] * 8 + [full2((8, d)), full2((C, C)), full2((d, LANES)), full2((LANES, d))],
        out_specs=chunk,
        out_shape=jax.ShapeDtypeStruct((ntok, d), BF16),
        scratch_shapes=[pltpu.VMEM((d // gw, gw, gw), F32)],
        compiler_params=_cparams("parallel", "arbitrary"),
        name="rwkv_scan",
    )(r, lw, k2, v, kk, b, g, bonus, gn, tri, sel, sel_t)


def _s5_in_kernel(x_ref, mod_ref, w_ref, u_ref):
    h = _modulate(x_ref[...], mod_ref[0], 0, 1)
    u_ref[...] = _dot(h, w_ref[...])


def _s5_scan_kernel(u_ref, krow_ref, wst_ref, wout_ref, lam_ref, y_ref, kbig, lhs_sc, e_sc, xp_sc, xstate):
    q = pl.program_id(1)
    bsz, seq_rows, _ = u_ref.shape
    L = S5_CHUNK
    nq = seq_rows // L
    rows = bsz * nq

    @pl.when(q == 0)
    def _():
        xstate[...] = jnp.zeros(xstate.shape, F32)
        zero = jnp.zeros((LANES, LANES), BF16)
        for t in range(L):
            for s in range(L):
                blk = krow_ref[0, :, (t - s) * LANES:(t - s + 1) * LANES] if t >= s else zero
                kbig[t // 2, s * LANES:(s + 1) * LANES, (t % 2) * LANES:(t % 2 + 1) * LANES] = blk

    for s in range(L):
        lhs_sc[:, s * LANES:(s + 1) * LANES] = u_ref[:, pl.ds(s, nq, stride=L), :].reshape(rows, LANES).astype(BF16)
    e = jnp.dot(lhs_sc[...], wst_ref[0], preferred_element_type=F32)
    nslab = e_sc.shape[0]
    for j in range(nslab):
        e_sc[j] = e[:, j * LANES:(j + 1) * LANES]
    lam = lam_ref[0]
    l0 = lam[0:1]
    l1 = lam[1:2]
    half = xstate.shape[1] // 2

    def step(n, xst):
        sel = pl.ds(n, bsz, stride=nq)
        for j in range(nslab):
            xp_sc[j, sel, :] = xst[:, j * LANES:(j + 1) * LANES]
        e_n = jnp.concatenate([e_sc[j, sel, :] for j in range(nslab)], axis=1)
        return xst * l0 + pltpu.roll(xst, half, axis=1) * l1 + e_n

    xstate[...] = lax.fori_loop(0, nq, step, xstate[...])
    xprev = jnp.concatenate([xp_sc[j] for j in range(nslab)], axis=1).astype(BF16)

    def out_pair(i, carry):
        y2 = (jnp.dot(lhs_sc[...], kbig[i], preferred_element_type=F32)
              + jnp.dot(xprev, wout_ref[0, i], preferred_element_type=F32))
        for k in range(2):
            y_ref[:, pl.ds(2 * i + k, nq, stride=L), :] = y2[:, k * LANES:(k + 1) * LANES].reshape(bsz, nq, LANES)
        return carry

    lax.fori_loop(0, L // 2, out_pair, 0)


def _s5_glu_kernel(y_ref, u_ref, d_ref, w_ref, o_ref):
    y = y_ref[...] + d_ref[...] * u_ref[...]
    ge = 0.5 * y * (1.0 + jnp.tanh(math.sqrt(2.0 / math.pi) * (y + 0.044715 * (y * y * y))))
    ab = _dot(ge, w_ref[...])
    o_ref[...] = (ab[:, :D_MODEL] * _sigmoid(ab[:, D_MODEL:])).astype(BF16)


def _s5_operators(a_re, a_im, log_dt, b_re, b_im, c_re, c_im):
    hp = lax.Precision.HIGHEST
    L = S5_CHUNK
    a_re, a_im = a_re.astype(F32), a_im.astype(F32)
    dt = jnp.exp(log_dt.astype(F32))[:, None]
    mag = jnp.exp(a_re * dt)
    ab_re, ab_im = mag * jnp.cos(a_im * dt), mag * jnp.sin(a_im * dt)
    den = jnp.square(a_re) + jnp.square(a_im)
    f_re = ((ab_re - 1.0) * a_re + ab_im * a_im) / den
    f_im = (ab_im * a_re - (ab_re - 1.0) * a_im) / den
    b_re, b_im = b_re.astype(F32), b_im.astype(F32)
    bb_re = f_re[..., None] * b_re - f_im[..., None] * b_im
    bb_im = f_re[..., None] * b_im + f_im[..., None] * b_re
    tau = jnp.arange(L + 1, dtype=F32)[:, None, None]
    pmag = jnp.exp(tau * (a_re * dt)[None])
    pw_re = pmag * jnp.cos(tau * (a_im * dt)[None])
    pw_im = pmag * jnp.sin(tau * (a_im * dt)[None])
    c_re, c_im = c_re.astype(F32), c_im.astype(F32)
    cl_re = c_re[None] * pw_re[:, :, None, :] - c_im[None] * pw_im[:, :, None, :]
    cl_im = c_re[None] * pw_im[:, :, None, :] + c_im[None] * pw_re[:, :, None, :]
    kern = (jnp.einsum('tgcp,gpd->tgcd', cl_re, bb_re, precision=hp)
            - jnp.einsum('tgcp,gpd->tgcd', cl_im, bb_im, precision=hp))
    pk = S5_PACK
    npk = S5_N_GROUPS // pk
    eye = jnp.eye(pk, dtype=F32)
    krow = jnp.einsum('tGacd,ab->Gadtbc', kern[:L].reshape(L, npk, pk, S5_GROUP, S5_GROUP), eye)
    krow = krow.reshape(npk, pk * S5_GROUP, L * pk * S5_GROUP)
    rev_re = pw_re[L - 1 - jnp.arange(L)]
    rev_im = pw_im[L - 1 - jnp.arange(L)]
    st_re = rev_re[:, :, :, None] * bb_re[None] - rev_im[:, :, :, None] * bb_im[None]
    st_im = rev_re[:, :, :, None] * bb_im[None] + rev_im[:, :, :, None] * bb_re[None]
    st = jnp.stack([st_re, st_im]).reshape(2, L, npk, pk, S5_STATE, S5_GROUP)
    wst = jnp.einsum('qsGapd,ab->Gsadqbp', st, eye).reshape(npk, L * pk * S5_GROUP, 2 * pk * S5_STATE)
    cl = jnp.stack([cl_re[1:], -cl_im[1:]]).reshape(2, L, npk, pk, S5_GROUP, S5_STATE)
    wout = jnp.einsum('qtGacp,ab->Gqaptbc', cl, eye).reshape(npk, 2 * pk * S5_STATE, L // 2, 2 * pk * S5_GROUP)
    wout = jnp.transpose(wout, (0, 2, 1, 3))
    lr = pw_re[L].reshape(npk, pk * S5_STATE)
    li = pw_im[L].reshape(npk, pk * S5_STATE)
    lam = jnp.stack([jnp.concatenate([lr, lr], -1), jnp.concatenate([-li, li], -1)], axis=1)
    return krow.astype(BF16), wst.astype(BF16), wout.astype(BF16), lam


def _s5_mixer(x2, mod_l, bsz, seq, w_in, a_re, a_im, log_dt, b_re, b_im, c_re, c_im, d_skip, w_glu):
    ntok = x2.shape[0]
    tiles = ntok // ROW_TILE
    tps = seq // ROW_TILE
    d = D_MODEL
    u = pl.pallas_call(
        _s5_in_kernel,
        grid=(tiles,),
        in_specs=[_row_spec(d), _mod_spec(tps), _full_spec((d, d))],
        out_specs=_row_spec(d),
        out_shape=jax.ShapeDtypeStruct((ntok, d), F32),
        compiler_params=_cparams("parallel"),
        name="s5_in",
    )(x2, mod_l, w_in.astype(BF16))

    L = S5_CHUNK
    npk = S5_N_GROUPS // S5_PACK
    seq_rows = min(S5_STEP_CHUNKS * L, seq)
    nq = seq_rows // L
    width = L * LANES
    nstate = 2 * S5_PACK * S5_STATE
    krow, wst, wout, lam = _s5_operators(a_re, a_im, log_dt, b_re, b_im, c_re, c_im)
    blk = pl.BlockSpec((bsz, seq_rows, LANES), lambda g, q: (0, q, g))
    pack = lambda *shape: pl.BlockSpec((1,) + shape, lambda g, q: (g,) + (0,) * len(shape))
    y = pl.pallas_call(
        _s5_scan_kernel,
        grid=(npk, seq // seq_rows),
        in_specs=[blk, pack(LANES, width), pack(width, nstate), pack(L // 2, nstate, 2 * LANES), pack(2, nstate)],
        out_specs=blk,
        out_shape=jax.ShapeDtypeStruct((bsz, seq, d), F32),
        scratch_shapes=[pltpu.VMEM((L // 2, width, 2 * LANES), BF16), pltpu.VMEM((bsz * nq, width), BF16),
                        pltpu.VMEM((nstate // LANES, bsz * nq, LANES), F32),
                        pltpu.VMEM((nstate // LANES, bsz * nq, LANES), F32), pltpu.VMEM((bsz, nstate), F32)],
        compiler_params=_cparams("parallel", "arbitrary"),
        name="s5_scan",
    )(u.reshape(bsz, seq, d), krow, wst, wout, lam).reshape(ntok, d)

    return pl.pallas_call(
        _s5_glu_kernel,
        grid=(tiles,),
        in_specs=[_row_spec(d), _row_spec(d), _full_spec((1, d)), _full_spec((d, 2 * d))],
        out_specs=_row_spec(d),
        out_shape=jax.ShapeDtypeStruct((ntok, d), BF16),
        compiler_params=_cparams("parallel"),
        name="s5_glu",
    )(y, u, d_skip.astype(F32).reshape(1, d), w_glu.astype(BF16))


def _post_kernel(x_ref, y_ref, w_ref, mod_ref, ln_ref, rw_ref, rb_ref, tris_ref,
                 xn_ref, h2_ref, rout_ref, cnt_ref, run):
    i = pl.program_id(0)
    tm = x_ref.shape[0]

    @pl.when(i == 0)
    def _():
        run[...] = jnp.zeros(run.shape, F32)

    m = mod_ref[0]
    ln = ln_ref[...]
    y = jnp.dot(y_ref[...], w_ref[...], preferred_element_type=F32)
    xn = _layer_norm(DEEPNORM_ALPHA * x_ref[...] + (1.0 + m[2:3]) * y, ln[0:1], ln[1:2])
    xn_ref[...] = xn
    h2 = _modulate(xn, m, 3, 4)
    h2_ref[...] = h2

    hh, hl = _split(h2)
    logits = _dot_nt(rw_ref[0], hh) + _dot_nt(rw_ref[0], hl) + _dot_nt(rw_ref[1], hh)
    ex = jnp.exp(logits - jnp.max(logits, axis=0, keepdims=True))
    probs = ex / jnp.sum(ex, axis=0, keepdims=True)
    sel = probs + rb_ref[...]

    def first_max(vals):
        best = vals[0]
        for v in vals[1:]:
            best = jnp.maximum(best, v)
        idx = jnp.full(best.shape, len(vals) - 1, jnp.int32)
        for j in range(len(vals) - 2, -1, -1):
            idx = jnp.where(vals[j] == best, j, idx)
        return best, idx

    scores, firsts, seconds = [], [], []
    for g in range(N_EXPERT_GROUPS):
        s = [sel[g * EXPERTS_PER_GROUP + j:g * EXPERTS_PER_GROUP + j + 1, :] for j in range(EXPERTS_PER_GROUP)]
        v1, i1 = first_max(s)
        v2, i2 = first_max([jnp.where(i1 == j, NEG_BIG, s[j]) for j in range(EXPERTS_PER_GROUP)])
        scores.append(v1 + v2)
        firsts.append(i1)
        seconds.append(i2)
    _, grp = first_max(scores)
    e0 = jnp.zeros_like(grp)
    e1 = jnp.zeros_like(grp)
    for g in range(N_EXPERT_GROUPS):
        e0 = jnp.where(grp == g, g * EXPERTS_PER_GROUP + firsts[g], e0)
        e1 = jnp.where(grp == g, g * EXPERTS_PER_GROUP + seconds[g], e1)

    eidx = lax.broadcasted_iota(jnp.int32, (N_EXPERTS, tm), 0)
    hit0 = eidx == e0
    hit1 = eidx == e1
    w0 = jnp.sum(jnp.where(hit0, probs, 0.0), axis=0, keepdims=True)
    w1 = jnp.sum(jnp.where(hit1, probs, 0.0), axis=0, keepdims=True)
    wsum = w0 + w1
    onehot = hit0.astype(F32) + hit1.astype(F32)
    before = _dot(onehot, tris_ref[...]) + run[...]
    r0 = jnp.sum(jnp.where(hit0, before, 0.0), axis=0, keepdims=True)
    r1 = jnp.sum(jnp.where(hit1, before, 0.0), axis=0, keepdims=True)
    run[...] = run[...] + jnp.sum(onehot, axis=1, keepdims=True)
    cnt_ref[...] = jnp.broadcast_to(run[...], cnt_ref.shape)

    rid = lax.broadcasted_iota(jnp.int32, (8, tm), 0)
    rows = (e0.astype(F32), e1.astype(F32), w0 / wsum, w1 / wsum, r0, r1)
    out = jnp.zeros((8, tm), F32)
    for j, val in enumerate(rows):
        out = jnp.where(rid == j, val, out)
    rout_ref[...] = out


def _post(x2, y_in, w_out, mod_l, ln_g, ln_b, rw, rb, seq):
    ntok = x2.shape[0]
    tiles = ntok // ROW_TILE
    tps = seq // ROW_TILE
    d = D_MODEL
    kin = y_in.shape[1]
    ln = _pad_to(jnp.stack([ln_g, ln_b]).astype(F32), 0, 8)
    tris = jnp.triu(jnp.ones((ROW_TILE, ROW_TILE), BF16), 1)
    return pl.pallas_call(
        _post_kernel,
        grid=(tiles,),
        in_specs=[_row_spec(d), _row_spec(kin), _full_spec((kin, d)), _mod_spec(tps), _full_spec((8, d)),
                  _full_spec((2, N_EXPERTS, d)), _full_spec((N_EXPERTS, 1)), _full_spec((ROW_TILE, ROW_TILE))],
        out_specs=[_row_spec(d), _row_spec(d), pl.BlockSpec((8, ROW_TILE), lambda i: (0, i)),
                   _full_spec((N_EXPERTS, LANES))],
        out_shape=[jax.ShapeDtypeStruct((ntok, d), F32), jax.ShapeDtypeStruct((ntok, d), F32),
                   jax.ShapeDtypeStruct((8, ntok), F32), jax.ShapeDtypeStruct((N_EXPERTS, LANES), F32)],
        scratch_shapes=[pltpu.VMEM((N_EXPERTS, 1), F32)],
        compiler_params=_cparams("arbitrary"),
        name="post",
    )(x2, y_in, w_out.astype(BF16), mod_l, ln, rw, rb, tris)


def _ffn_kernel(be_ref, tok_ref, tokn_ref, h_ref, wu_ref, wd_ref, o_ref, xbuf, wub, wdb, sem):
    i = pl.program_id(0)
    last = pl.num_programs(0) - 1
    bm = o_ref.shape[0]
    slot = lax.rem(i, 2)
    rows = 128

    def gather(tok, slot_, t):
        return pltpu.make_async_copy(h_ref.at[pl.ds(tok[0, 0, t], 1)], xbuf.at[slot_, pl.ds(t, 1)], sem.at[slot_])

    @pl.when(i == 0)
    def _():
        for t in range(bm):
            gather(tok_ref, 0, t).start()

    prev = be_ref[jnp.maximum(i - 1, 0)]

    @pl.when((i == 0) | (be_ref[i] != prev))
    def _():
        def cast(j, carry):
            sl = pl.ds(pl.multiple_of(j * rows, rows), rows)
            wub[sl, :] = wu_ref[0, 0, sl, :].astype(BF16)
            wdb[sl, :] = wd_ref[0, 0, sl, :].astype(BF16)
            return carry
        lax.fori_loop(0, D_MODEL // rows, cast, 0)

    for t in range(bm):
        gather(tokn_ref, 1 - slot, t).start()
    for t in range(bm):
        gather(tok_ref, slot, t).wait()
    hid = jnp.dot(xbuf[slot].astype(BF16), wub[...], preferred_element_type=F32)
    act = _silu(hid[:, :D_MODEL]) * hid[:, D_MODEL:]
    o_ref[...] = jnp.dot(act.astype(BF16), wdb[...], preferred_element_type=F32)

    @pl.when(i == last)
    def _():
        for t in range(bm):
            gather(tokn_ref, 1 - slot, t).wait()


def _combine_kernel(dest_ref, destn_ref, xn_ref, wc_ref, mod_ref, ln_ref, ys_ref, o_ref, buf, sem):
    i = pl.program_id(0)
    last = pl.num_programs(0) - 1
    tm = xn_ref.shape[0]
    slot = lax.rem(i, 2)

    def gather(dref, slot_, k, t):
        return pltpu.make_async_copy(ys_ref.at[pl.ds(dref[0, 0, k * tm + t], 1)], buf.at[slot_, k, pl.ds(t, 1)],
                                     sem.at[slot_])

    def for_all(fn):
        for k in range(TOP_K):
            for t in range(tm):
                fn(k, t)

    @pl.when(i == 0)
    def _():
        for_all(lambda k, t: gather(dest_ref, 0, k, t).start())

    for_all(lambda k, t: gather(destn_ref, 1 - slot, k, t).start())
    for_all(lambda k, t: gather(dest_ref, slot, k, t).wait())
    m = mod_ref[0]
    ln = ln_ref[...]
    wc = wc_ref[...]
    y = wc[:, 0:1] * buf[slot, 0] + wc[:, 1:2] * buf[slot, 1]
    o_ref[...] = _layer_norm(DEEPNORM_ALPHA * xn_ref[...] + (1.0 + m[5:6]) * y, ln[0:1], ln[1:2])

    @pl.when(i == last)
    def _():
        for_all(lambda k, t: gather(destn_ref, 1 - slot, k, t).wait())


def _moe(xn, h2, rout, cnt, mod_l, ln_g, ln_b, w_up, w_down, layer, seq):
    ntok = xn.shape[0]
    d = D_MODEL
    tiles = ntok // ROW_TILE
    tps = seq // ROW_TILE
    bm = MOE_ROWS
    n_assign = ntok * TOP_K
    n_blocks = n_assign // bm + N_EXPERTS
    n_pad = n_blocks * bm

    e0 = rout[0].astype(jnp.int32)
    e1 = rout[1].astype(jnp.int32)
    counts = cnt[:, 0].astype(jnp.int32)
    padded = (counts + bm - 1) // bm * bm
    pad_end = jnp.cumsum(padded)
    pad_start = pad_end - padded
    dest0 = pad_start[e0] + rout[4].astype(jnp.int32)
    dest1 = pad_start[e1] + rout[5].astype(jnp.int32)
    dest = jnp.concatenate([dest0.reshape(tiles, 1, ROW_TILE), dest1.reshape(tiles, 1, ROW_TILE)], axis=-1)
    tok = jnp.arange(ntok, dtype=jnp.int32)
    row_tok = jnp.zeros((n_pad,), jnp.int32).at[jnp.concatenate([dest0, dest1])].set(
        jnp.concatenate([tok, tok]), unique_indices=True).reshape(n_blocks, 1, bm)
    block_start = jnp.arange(n_blocks, dtype=jnp.int32) * bm
    block_expert = jnp.minimum(jnp.sum((block_start[:, None] >= pad_end[None, :]).astype(jnp.int32), axis=1),
                               N_EXPERTS - 1)
    wcol = jnp.transpose(rout[2:4])

    tok_spec = lambda nxt: pl.BlockSpec((1, 1, bm), lambda i, be: (jnp.minimum(i + nxt, n_blocks - 1), 0, 0),
                                        memory_space=pltpu.SMEM)
    ys = pl.pallas_call(
        _ffn_kernel,
        grid_spec=pltpu.PrefetchScalarGridSpec(
            num_scalar_prefetch=1,
            grid=(n_blocks,),
            in_specs=[tok_spec(0), tok_spec(1), pl.BlockSpec(memory_space=pl.ANY),
                      pl.BlockSpec((1, 1, d, 2 * d), lambda i, be: (layer, be[i], 0, 0)),
                      pl.BlockSpec((1, 1, d, d), lambda i, be: (layer, be[i], 0, 0))],
            out_specs=pl.BlockSpec((bm, d), lambda i, be: (i, 0)),
            scratch_shapes=[pltpu.VMEM((2, bm, d), F32), pltpu.VMEM((d, 2 * d), BF16), pltpu.VMEM((d, d), BF16),
                            pltpu.SemaphoreType.DMA((2,))],
        ),
        out_shape=jax.ShapeDtypeStruct((n_pad, d), F32),
        compiler_params=_cparams("arbitrary"),
        name="moe_ffn",
    )(block_expert, row_tok, row_tok, h2, w_up, w_down)

    ln = _pad_to(jnp.stack([ln_g, ln_b]).astype(F32), 0, 8)
    dest_spec = lambda nxt: pl.BlockSpec((1, 1, 2 * ROW_TILE), lambda i: (jnp.minimum(i + nxt, tiles - 1), 0, 0),
                                         memory_space=pltpu.SMEM)
    return pl.pallas_call(
        _combine_kernel,
        grid=(tiles,),
        in_specs=[dest_spec(0), dest_spec(1), _row_spec(d), pl.BlockSpec((ROW_TILE, 2), lambda i: (i, 0)),
                  _mod_spec(tps), _full_spec((8, d)), pl.BlockSpec(memory_space=pl.ANY)],
        out_specs=_row_spec(d),
        out_shape=jax.ShapeDtypeStruct((ntok, d), F32),
        scratch_shapes=[pltpu.VMEM((2, TOP_K, ROW_TILE, d), F32), pltpu.SemaphoreType.DMA((2,))],
        compiler_params=_cparams("arbitrary"),
        name="moe_combine",
    )(dest, dest, xn, wcol, mod_l, ln, ys)


def kernel(x, c, ada_w, ada_b, ln1_g, ln1_b, ln2_g, ln2_b, ssd_w_in, ssd_conv_w, ssd_conv_b, ssd_dt_bias, ssd_a_log, ssd_d, ssd_norm_g, ssd_w_out, gla_w_in, gla_w_gate, gla_b_gate, gla_norm_g, gla_w_out, rwkv_mu, rwkv_w_in, rwkv_w0, rwkv_w_w1, rwkv_w_w2, rwkv_a0, rwkv_w_a1, rwkv_w_a2, rwkv_w_g1, rwkv_w_g2, rwkv_k_k, rwkv_k_a, rwkv_r_k, rwkv_gn_g, rwkv_gn_b, rwkv_w_out, s5_w_in, s5_a_re, s5_a_im, s5_log_dt, s5_b_re, s5_b_im, s5_c_re, s5_c_im, s5_d, s5_w_glu, s5_w_out, moe_w_up, moe_w_down, router_w, router_b):
    bsz, seq, d = x.shape
    assert d == D_MODEL and seq % ROW_TILE == 0 and seq % SSD_CHUNK == 0
    depth = ada_w.shape[0]
    mod = _adaln_mod(c, ada_w, ada_b).reshape(depth, bsz, 6, d)
    rw = _hilo(jnp.transpose(router_w))
    rb = router_b.astype(F32).reshape(N_EXPERTS, 1)
    x2 = x.reshape(bsz * seq, d)
    for i in range(depth):
        kind, j = i % 4, i // 4
        mod_l = mod[i]
        if kind == 0:
            y = _ssd_mixer(x2, mod_l, bsz, seq, ssd_w_in[j], ssd_conv_w[j], ssd_conv_b[j], ssd_dt_bias[j],
                           ssd_a_log[j], ssd_d[j], ssd_norm_g[j])
            w_out = ssd_w_out[j]
        elif kind == 1:
            y = _gla_mixer(x2, mod_l, bsz, seq, gla_w_in[j], gla_w_gate[j], gla_b_gate[j], gla_norm_g[j])
            w_out = gla_w_out[j]
        elif kind == 2:
            y = _rwkv_mixer(x2, mod_l, bsz, seq, rwkv_mu[j], rwkv_w_in[j], rwkv_w0[j], rwkv_w_w1[j], rwkv_w_w2[j],
                            rwkv_a0[j], rwkv_w_a1[j], rwkv_w_a2[j], rwkv_w_g1[j], rwkv_w_g2[j], rwkv_k_k[j],
                            rwkv_k_a[j], rwkv_r_k[j], rwkv_gn_g[j], rwkv_gn_b[j])
            w_out = rwkv_w_out[j]
        else:
            y = _s5_mixer(x2, mod_l, bsz, seq, s5_w_in[j], s5_a_re[j], s5_a_im[j], s5_log_dt[j], s5_b_re[j],
                          s5_b_im[j], s5_c_re[j], s5_c_im[j], s5_d[j], s5_w_glu[j])
            w_out = s5_w_out[j]
        xn, h2, rout, cnt = _post(x2, y, w_out, mod_l, ln1_g[i], ln1_b[i], rw, rb, seq)
        x2 = _moe(xn, h2, rout, cnt, mod_l, ln2_g[i], ln2_b[i], moe_w_up, moe_w_down, i, seq)
    return x2.reshape(bsz, seq, d)
```

```python
import functools
import math

import jax
import jax.numpy as jnp
from jax import lax
from jax.experimental import pallas as pl
from jax.experimental.pallas import tpu as pltpu

F32 = jnp.float32
BF16 = jnp.bfloat16

D_MODEL = 1024
DEPTH = 4
DEEPNORM_ALPHA = (2 * DEPTH) ** 0.25
LN_EPS = 1e-5
RMS_EPS = 1e-5

SSD_D_INNER = 2048
SSD_HEAD_DIM = 64
SSD_N_HEADS = 32
SSD_N_GROUPS = 4
SSD_D_STATE = 128
SSD_CONV = 4
SSD_CONV_DIM = SSD_D_INNER + 2 * SSD_N_GROUPS * SSD_D_STATE
SSD_CHUNK = 128

GLA_N_HEADS = 4
GLA_D_K = 512
GLA_D_V = 1024
GLA_GATE_RANK = 16
GLA_TAU = 16.0
GLA_SUB = 16
GLA_ROWS = 256

RWKV_HEAD_DIM = 64
RWKV_N_HEADS = 16
RWKV_GN_EPS = 64e-5
RWKV_CHUNK = 64
RWKV_GROUP = 4
RWKV_SEQS = 2

S5_GROUP = 16
S5_N_GROUPS = 64
S5_STATE = 64
S5_CHUNK = 16
S5_PACK = 8
S5_STEP_CHUNKS = 16

N_EXPERTS = 16
N_EXPERT_GROUPS = 4
EXPERTS_PER_GROUP = 4
TOP_K = 2
MOE_ROWS = 256

ROW_TILE = 256
POST_TILE = 1024
LANES = 128
VMEM_LIMIT = 56 * 1024 * 1024
NEG_BIG = -1e30


def _cparams(*sem):
    return pltpu.CompilerParams(dimension_semantics=sem, vmem_limit_bytes=VMEM_LIMIT)


def _dot(a, b):
    return jnp.dot(a.astype(BF16), b.astype(BF16), preferred_element_type=F32)


def _dot_nt(a, b):
    return lax.dot_general(a.astype(BF16), b.astype(BF16), (((1,), (1,)), ((), ())), preferred_element_type=F32)


def _dot_tn(a, b):
    return lax.dot_general(a.astype(BF16), b.astype(BF16), (((0,), (0,)), ((), ())), preferred_element_type=F32)


def _split(x):
    hi = x.astype(BF16)
    lo = (x - hi.astype(F32)).astype(BF16)
    return hi, lo


def _split3(x):
    h1 = x.astype(BF16)
    r1 = x - h1.astype(F32)
    h2 = r1.astype(BF16)
    h3 = (r1 - h2.astype(F32)).astype(BF16)
    return h1, h2, h3


def _dot_x2(a, b_exact):
    hi, lo = _split(a)
    return _dot(hi, b_exact) + _dot(lo, b_exact)


def _dot_x3(a, b_exact):
    h1, h2, h3 = _split3(a)
    return _dot(h1, b_exact) + _dot(h2, b_exact) + _dot(h3, b_exact)


def _dot_hl(a_hi, a_lo, b_hi, b_lo):
    return _dot(a_hi, b_hi) + _dot(a_lo, b_hi) + _dot(a_hi, b_lo)


def _sigmoid(x):
    return 1.0 / (1.0 + jnp.exp(-x))


def _silu(x):
    return x * _sigmoid(x)


def _softplus(x):
    return jnp.maximum(x, 0.0) + jnp.log1p(jnp.exp(-jnp.abs(x)))


def _hilo(w):
    hi, lo = _split(w.astype(F32))
    return jnp.stack([hi, lo])


def _pad_to(a, axis, size):
    pad = [(0, 0)] * a.ndim
    pad[axis] = (0, size - a.shape[axis])
    return jnp.pad(a, pad)


def _modulate(x, m, shift_row, scale_row):
    return x * (1.0 + m[scale_row:scale_row + 1]) + m[shift_row:shift_row + 1]


def _layer_norm(v, g, b):
    mu = jnp.mean(v, axis=-1, keepdims=True)
    vc = v - mu
    var = jnp.mean(vc * vc, axis=-1, keepdims=True)
    return vc * lax.rsqrt(var + LN_EPS) * g + b


def _mod_kernel(c_ref, w_ref, b_ref, o_ref):
    o_ref[0] = _dot(_silu(c_ref[...]), w_ref[0]) + b_ref[0]


def _adaln_mod(c, ada_w, ada_b):
    nl, d, n6 = ada_w.shape
    bsz = c.shape[0]
    tn = 512
    return pl.pallas_call(
        _mod_kernel,
        grid=(nl, n6 // tn),
        in_specs=[
            pl.BlockSpec((bsz, d), lambda l, j: (0, 0)),
            pl.BlockSpec((1, d, tn), lambda l, j: (l, 0, j)),
            pl.BlockSpec((1, 1, tn), lambda l, j: (l, 0, j)),
        ],
        out_specs=pl.BlockSpec((1, bsz, tn), lambda l, j: (l, 0, j)),
        out_shape=jax.ShapeDtypeStruct((nl, bsz, n6), F32),
        compiler_params=_cparams("parallel", "parallel"),
        name="adaln_mod",
    )(c, ada_w, ada_b.reshape(nl, 1, n6))


def _row_spec(width, tm=ROW_TILE):
    return pl.BlockSpec((tm, width), lambda i: (i, 0))


def _full_spec(shape):
    nd = len(shape)
    return pl.BlockSpec(shape, lambda *_: (0,) * nd)


def _mod_spec(tiles_per_seq):
    return pl.BlockSpec((1, 6, D_MODEL), lambda i: (i // tiles_per_seq, 0, 0))


def _tile_rows_spec(tm):
    return pl.BlockSpec((tm, D_MODEL // LANES, LANES), lambda i, *_: (i, 0, 0))


def _store_rows(ref, val):
    for k in range(val.shape[1] // LANES):
        ref[:, k, :] = val[:, k * LANES:(k + 1) * LANES]


def _load_rows(ref):
    return jnp.concatenate([ref[:, k, :] for k in range(ref.shape[1])], axis=1)


def _ssd_in_kernel(x_ref, mod_ref, wz_ref, wx_ref, wdt_ref, z_ref, xbc_ref, dt_ref):
    h = _modulate(x_ref[...], mod_ref[0], 0, 1)
    hb, hl = _split(h)
    z_ref[...] = jnp.dot(hb, wz_ref[...], preferred_element_type=F32).astype(BF16)
    xbc_ref[...] = jnp.dot(hb, wx_ref[...], preferred_element_type=F32)
    dt_ref[...] = _dot_hl(hb, hl, wdt_ref[0], wdt_ref[1])


def _ssd_scan_kernel(z_ref, xbc_ref, dt_ref, cw_ref, cb_ref, dtb_ref, an_ref, dsk_ref, ng_ref,
                     e_ref, tri_ref, triu_ref, o_ref, xpad, state):
    c = pl.program_id(1)
    L = z_ref.shape[0]
    nstate = SSD_D_STATE

    @pl.when(c == 0)
    def _():
        state[...] = jnp.zeros(state.shape, F32)
        xpad[0:8, :] = jnp.zeros((8, SSD_CONV_DIM), F32)

    @pl.when(c != 0)
    def _():
        xpad[0:8, :] = xpad[L:L + 8, :]

    xpad[8:8 + L, :] = xbc_ref[...]

    acc = cb_ref[...] + cw_ref[3:4, :] * xpad[8:8 + L, :]
    for j in range(1, SSD_CONV):
        acc = acc + cw_ref[3 - j:4 - j, :] * xpad[pl.ds(8 - j, L), :]
    act = _silu(acc)
    xs = act[:, :SSD_D_INNER]
    bm = act[:, SSD_D_INNER:SSD_D_INNER + SSD_N_GROUPS * nstate]
    cm = act[:, SSD_D_INNER + SSD_N_GROUPS * nstate:]

    dtv = _softplus(dt_ref[...] + dtb_ref[...])
    dta = dtv * an_ref[...]
    dhi, dlo = _split(dta)
    tri = tri_ref[...]
    triu = triu_ref[...]
    acum = _dot(tri, dhi) + _dot(tri, dlo)
    acum_t = _dot_tn(dhi, triu) + _dot_tn(dlo, triu)

    e = e_ref[...]
    acum_x = _dot_x3(acum, e)
    dt_x = _dot_x3(dtv, e)
    eac_x = jnp.exp(acum_x)
    last_x = acum_x[L - 1:L, :]
    elast_x = jnp.exp(last_x)
    xdt = xs * dt_x
    xw = xs * (jnp.exp(last_x - acum_x) * dt_x)

    row = lax.broadcasted_iota(jnp.int32, (L, L), 0)
    col = lax.broadcasted_iota(jnp.int32, (L, L), 1)
    causal = row >= col
    lane = lax.broadcasted_iota(jnp.int32, (L, LANES), 1)
    left = lane < SSD_HEAD_DIM

    pieces = []
    heads_per_group = SSD_N_HEADS // SSD_N_GROUPS
    for g in range(SSD_N_GROUPS):
        bg = bm[:, g * nstate:(g + 1) * nstate]
        cg = cm[:, g * nstate:(g + 1) * nstate]
        cb = _dot_nt(cg, bg)
        for p in range(heads_per_group // 2):
            h0 = g * heads_per_group + 2 * p
            pair = h0 // 2
            sl = slice(h0 * SSD_HEAD_DIM, h0 * SSD_HEAD_DIM + LANES)
            ms = []
            for hh in (h0, h0 + 1):
                seg = acum[:, hh:hh + 1] - acum_t[hh:hh + 1, :]
                ms.append((cb * jnp.exp(jnp.where(causal, seg, NEG_BIG))).astype(BF16))
            mcat = jnp.concatenate(ms, axis=1)
            xp = xdt[:, sl]
            xbd = jnp.concatenate([jnp.where(left, xp, 0.0), jnp.where(left, 0.0, xp)], axis=0)
            st = state[pair]
            y_pair = _dot(mcat, xbd) + _dot(cg, st) * eac_x[:, sl]
            state[pair] = st * elast_x[:, sl] + _dot_tn(bg, xw[:, sl])
            pieces.append(y_pair)
    y = jnp.concatenate(pieces, axis=1) + dsk_ref[...] * xs
    y = y * _silu(z_ref[...].astype(F32))
    gw = SSD_D_INNER // SSD_N_GROUPS
    outs = []
    for g in range(SSD_N_GROUPS):
        yg = y[:, g * gw:(g + 1) * gw]
        outs.append(yg * lax.rsqrt(jnp.mean(yg * yg, axis=-1, keepdims=True) + RMS_EPS))
    o_ref[...] = (jnp.concatenate(outs, axis=1) * ng_ref[...]).astype(BF16)


def _ssd_mixer(x2, mod_l, bsz, seq, w_in, conv_w, conv_b, dt_bias, a_log, d_skip, norm_g):
    ntok = x2.shape[0]
    tiles = ntok // ROW_TILE
    tps = seq // ROW_TILE
    wz = w_in[:, :SSD_D_INNER].astype(BF16)
    wx = w_in[:, SSD_D_INNER:SSD_D_INNER + SSD_CONV_DIM].astype(BF16)
    wdt = _hilo(_pad_to(w_in[:, SSD_D_INNER + SSD_CONV_DIM:], 1, LANES))
    z, xbc, dt = pl.pallas_call(
        _ssd_in_kernel,
        grid=(tiles,),
        in_specs=[_row_spec(D_MODEL), _mod_spec(tps), _full_spec(wz.shape), _full_spec(wx.shape),
                  _full_spec(wdt.shape)],
        out_specs=[_row_spec(SSD_D_INNER), _row_spec(SSD_CONV_DIM), _row_spec(LANES)],
        out_shape=[jax.ShapeDtypeStruct((ntok, SSD_D_INNER), BF16),
                   jax.ShapeDtypeStruct((ntok, SSD_CONV_DIM), F32),
                   jax.ShapeDtypeStruct((ntok, LANES), F32)],
        compiler_params=_cparams("parallel"),
        name="ssd_in",
    )(x2, mod_l, wz, wx, wdt)

    L = SSD_CHUNK
    nc = seq // L
    a_neg = _pad_to((-jnp.exp(a_log.astype(F32))).reshape(1, SSD_N_HEADS), 1, LANES)
    dtb = _pad_to(dt_bias.astype(F32).reshape(1, SSD_N_HEADS), 1, LANES)
    dsk = jnp.repeat(d_skip.astype(F32), SSD_HEAD_DIM).reshape(1, SSD_D_INNER)
    expand = (jnp.arange(LANES)[:, None] == (jnp.arange(SSD_D_INNER) // SSD_HEAD_DIM)[None, :]).astype(BF16)
    tri = jnp.tril(jnp.ones((L, L), BF16))
    triu = jnp.triu(jnp.ones((L, L), BF16))
    chunk = lambda w: pl.BlockSpec((L, w), lambda b, c: (b * nc + c, 0))
    full2 = lambda shape: pl.BlockSpec(shape, lambda b, c: (0, 0))
    return pl.pallas_call(
        _ssd_scan_kernel,
        grid=(bsz, nc),
        in_specs=[chunk(SSD_D_INNER), chunk(SSD_CONV_DIM), chunk(LANES),
                  full2((SSD_CONV, SSD_CONV_DIM)), full2((1, SSD_CONV_DIM)), full2((1, LANES)), full2((1, LANES)),
                  full2((1, SSD_D_INNER)), full2((1, SSD_D_INNER)), full2((LANES, SSD_D_INNER)),
                  full2((L, L)), full2((L, L))],
        out_specs=chunk(SSD_D_INNER),
        out_shape=jax.ShapeDtypeStruct((ntok, SSD_D_INNER), BF16),
        scratch_shapes=[pltpu.VMEM((L + 8, SSD_CONV_DIM), F32),
                        pltpu.VMEM((SSD_N_HEADS // 2, SSD_D_STATE, LANES), F32)],
        compiler_params=_cparams("parallel", "arbitrary"),
        name="ssd_scan",
    )(z, xbc, dt, conv_w.astype(F32), conv_b.astype(F32).reshape(1, -1), dtb, a_neg, dsk,
      norm_g.astype(F32).reshape(1, -1), expand, tri, triu)


def _gla_in_kernel(x_ref, mod_ref, wq_ref, wk_ref, wv_ref, wr_ref, wg_ref, wgate_ref, bgate_ref,
                   q_ref, k_ref, v_ref, r_ref, la_ref):
    h = _modulate(x_ref[...], mod_ref[0], 0, 1)
    hb, hl = _split(h)
    dk = GLA_D_K // GLA_N_HEADS
    q_ref[...] = jnp.dot(hb, wq_ref[...], preferred_element_type=F32) * (dk ** -0.5)
    k_ref[...] = jnp.dot(hb, wk_ref[...], preferred_element_type=F32)
    v_ref[...] = jnp.dot(hb, wv_ref[...], preferred_element_type=F32)
    r_ref[...] = jnp.dot(hb, wr_ref[...], preferred_element_type=F32).astype(BF16)
    g_low = _dot_hl(hb, hl, wg_ref[0], wg_ref[1])
    gh, gl = _split(g_low)
    gz = _dot_hl(gh, gl, wgate_ref[0], wgate_ref[1]) + bgate_ref[...]
    la_ref[...] = (jnp.minimum(gz, 0.0) - jnp.log1p(jnp.exp(-jnp.abs(gz)))) * (1.0 / GLA_TAU)


def _gla_scan_kernel(q_ref, k_ref, v_ref, r_ref, la_ref, ng_ref, tri_ref, o_ref, state):
    c = pl.program_id(1)
    rows = q_ref.shape[0]
    sub = GLA_SUB
    nsub = rows // sub
    dk = GLA_D_K // GLA_N_HEADS
    dv = GLA_D_V // GLA_N_HEADS

    @pl.when(c == 0)
    def _():
        state[...] = jnp.zeros(state.shape, F32)

    q = q_ref[...]
    k = k_ref[...]
    v = v_ref[...]
    lhi, llo = _split(la_ref[...])
    tri = tri_ref[...]
    bcum = _dot(tri, lhi) + _dot(tri, llo)

    def sub_row(a, j):
        a3 = a.reshape(nsub, sub, a.shape[-1])
        return jnp.broadcast_to(a3[:, j:j + 1, :], a3.shape).reshape(a.shape)

    last = sub_row(bcum, sub - 1)
    qdec = q * jnp.exp(bcum)
    kdec = k * jnp.exp(last - bcum)
    elast = jnp.exp(last)

    lane = lax.broadcasted_iota(jnp.int32, (rows, LANES), 1)
    local_row = lax.broadcasted_iota(jnp.int32, (rows, LANES), 0) & (sub - 1)
    att = jnp.zeros((rows, LANES), F32)
    for h in range(GLA_N_HEADS):
        ks = slice(h * dk, (h + 1) * dk)
        qh, kh, bh = q[:, ks], k[:, ks], bcum[:, ks]
        for j in range(sub):
            ex = jnp.exp(jnp.minimum(bh - sub_row(bh, j), 0.0))
            a = jnp.sum(qh * sub_row(kh, j) * ex, axis=-1, keepdims=True)
            att = jnp.where(lane == h * sub + j, a, att)
    att = jnp.where((lane & (sub - 1)) <= local_row, att, 0.0)
    ahi, alo = _split(att)

    ng = ng_ref[...]
    value_head = lax.broadcasted_iota(jnp.int32, (sub, GLA_D_V), 1) >> (dv.bit_length() - 1)
    for s in range(nsub):
        rs = slice(s * sub, (s + 1) * sub)
        vs = v[rs]
        vbd = jnp.concatenate([jnp.where(value_head == h, vs, 0.0) for h in range(GLA_N_HEADS)], axis=0)
        used = GLA_N_HEADS * sub
        o_intra = _dot(ahi[rs, :used], vbd) + _dot(alo[rs, :used], vbd)
        outs = []
        for h in range(GLA_N_HEADS):
            ks = slice(h * dk, (h + 1) * dk)
            st = state[h]
            oh = o_intra[:, h * dv:(h + 1) * dv] + _dot_nt(qdec[rs, ks], st)
            state[h] = st * elast[s * sub:s * sub + 1, ks] + _dot_tn(vs[:, h * dv:(h + 1) * dv], kdec[rs, ks])
            outs.append(oh * lax.rsqrt(jnp.mean(oh * oh, axis=-1, keepdims=True) + RMS_EPS))
        on = jnp.concatenate(outs, axis=1) * ng
        o_ref[rs, :] = (on * _silu(r_ref[rs, :].astype(F32))).astype(BF16)


def _gla_mixer(x2, mod_l, bsz, seq, w_in, w_gate, b_gate, norm_g):
    ntok = x2.shape[0]
    tiles = ntok // ROW_TILE
    tps = seq // ROW_TILE
    o1, o2, o3 = GLA_D_K, 2 * GLA_D_K, 2 * GLA_D_K + GLA_D_V
    o4 = o3 + GLA_D_V
    wq, wk = w_in[:, :o1].astype(BF16), w_in[:, o1:o2].astype(BF16)
    wv, wr = w_in[:, o2:o3].astype(BF16), w_in[:, o3:o4].astype(BF16)
    wg = _hilo(_pad_to(w_in[:, o4:], 1, LANES))
    wgate = _hilo(_pad_to(w_gate, 0, LANES))
    q, k, v, r, la = pl.pallas_call(
        _gla_in_kernel,
        grid=(tiles,),
        in_specs=[_row_spec(D_MODEL), _mod_spec(tps), _full_spec(wq.shape), _full_spec(wk.shape),
                  _full_spec(wv.shape), _full_spec(wr.shape), _full_spec(wg.shape), _full_spec(wgate.shape),
                  _full_spec((1, GLA_D_K))],
        out_specs=[_row_spec(GLA_D_K), _row_spec(GLA_D_K), _row_spec(GLA_D_V), _row_spec(GLA_D_V),
                   _row_spec(GLA_D_K)],
        out_shape=[jax.ShapeDtypeStruct((ntok, GLA_D_K), F32), jax.ShapeDtypeStruct((ntok, GLA_D_K), F32),
                   jax.ShapeDtypeStruct((ntok, GLA_D_V), F32), jax.ShapeDtypeStruct((ntok, GLA_D_V), BF16),
                   jax.ShapeDtypeStruct((ntok, GLA_D_K), F32)],
        compiler_params=_cparams("parallel"),
        name="gla_in",
    )(x2, mod_l, wq, wk, wv, wr, wg, wgate, b_gate.astype(F32).reshape(1, -1))

    rows = GLA_ROWS
    nc = seq // rows
    chunk = lambda w: pl.BlockSpec((rows, w), lambda b, c: (b * nc + c, 0))
    full2 = lambda shape: pl.BlockSpec(shape, lambda b, c: (0, 0))
    idx = jnp.arange(rows)
    tri = ((idx[:, None] >= idx[None, :]) & (idx[:, None] // GLA_SUB == idx[None, :] // GLA_SUB)).astype(BF16)
    return pl.pallas_call(
        _gla_scan_kernel,
        grid=(bsz, nc),
        in_specs=[chunk(GLA_D_K), chunk(GLA_D_K), chunk(GLA_D_V), chunk(GLA_D_V), chunk(GLA_D_K),
                  full2((1, GLA_D_V)), full2((rows, rows))],
        out_specs=chunk(GLA_D_V),
        out_shape=jax.ShapeDtypeStruct((ntok, GLA_D_V), BF16),
        scratch_shapes=[pltpu.VMEM((GLA_N_HEADS, GLA_D_V // GLA_N_HEADS, GLA_D_K // GLA_N_HEADS), F32)],
        compiler_params=_cparams("parallel", "arbitrary"),
        name="gla_scan",
    )(q, k, v, r, la, norm_g.astype(F32).reshape(1, -1), tri)


def _head_sum(x, sel, sel_t):
    return _dot_x3(_dot_x2(x, sel), sel_t)


def _rwkv_in_kernel(x_ref, mod_ref, mu_ref, wr_ref, wk_ref, wv_ref, ww1_ref, ww2_ref, wa1_ref, wa2_ref,
                    wg1_ref, wg2_ref, vec_ref, sel_ref, selt_ref,
                    r_ref, lw_ref, k2_ref, v_ref, kk_ref, b_ref, g_ref, bonus_ref, hp, *, tiles_per_seq):
    i = pl.program_id(0)
    tm = x_ref.shape[0]
    h = _modulate(x_ref[...], mod_ref[0], 0, 1)

    @pl.when(i % tiles_per_seq == 0)
    def _():
        hp[0:8, :] = jnp.zeros((8, D_MODEL), F32)

    @pl.when(i % tiles_per_seq != 0)
    def _():
        hp[0:8, :] = hp[tm:tm + 8, :]

    hp[8:8 + tm, :] = h
    xx = hp[pl.ds(7, tm), :] - h
    mu = mu_ref[...]
    xr, xw, xk, xv, xa, xg = (h + xx * mu[j:j + 1] for j in range(6))
    vec = vec_ref[...]
    w0, a0, k_k, k_a, r_k = (vec[j:j + 1] for j in range(5))
    sel = sel_ref[...]
    sel_t = selt_ref[...]

    r = _dot(xr, wr_ref[...])
    k = _dot(xk, wk_ref[...])
    v = _dot(xv, wv_ref[...])
    wlog = -_softplus(-(w0 + _dot(jnp.tanh(_dot(xw, ww1_ref[...])), ww2_ref[...]))) - 0.5
    a = _sigmoid(a0 + _dot(_dot(xa, wa1_ref[...]), wa2_ref[...]))
    g = _dot(_sigmoid(_dot(xg, wg1_ref[...])), wg2_ref[...])
    kx = k * k_k
    norm = jnp.sqrt(_head_sum(kx * kx, sel, sel_t))
    kk = kx / jnp.maximum(norm, 1e-12)
    k2 = k * (1.0 + (a - 1.0) * k_a)
    bonus = _head_sum(r * k2 * r_k, sel, sel_t) * v

    r_ref[...] = r.astype(BF16)
    lw_ref[...] = -jnp.exp(wlog)
    k2_ref[...] = k2.astype(BF16)
    v_ref[...] = v.astype(BF16)
    kk_ref[...] = kk.astype(BF16)
    b_ref[...] = (kk * a).astype(BF16)
    g_ref[...] = g.astype(BF16)
    bonus_ref[...] = bonus.astype(BF16)


def _rwkv_scan_kernel(r_ref, lw_ref, k2_ref, v_ref, kk_ref, b_ref, g_ref, bonus_ref, gn_ref, tri_ref, sel_ref,
                      selt_ref, o_ref, state):
    c = pl.program_id(1)
    C = RWKV_CHUNK
    n = RWKV_HEAD_DIM
    gw = RWKV_GROUP * n
    ngroups = D_MODEL // gw

    @pl.when(c == 0)
    def _():
        state[...] = jnp.zeros(state.shape, F32)

    tri = tri_ref[...]
    ri = lax.broadcasted_iota(jnp.int32, (gw, gw), 0)
    ci = lax.broadcasted_iota(jnp.int32, (gw, gw), 1)
    shift = n.bit_length() - 1
    same = (ri >> shift) == (ci >> shift)
    strict = same & ((ri & (C - 1)) > (ci & (C - 1)))
    incl = same & ((ri & (C - 1)) >= (ci & (C - 1)))
    lane_head = lax.broadcasted_iota(jnp.int32, (C, gw), 1) >> shift

    def expand(a):
        return jnp.concatenate([jnp.where(lane_head == hh, a, 0.0) for hh in range(RWKV_GROUP)], axis=0)

    def collapse(a):
        out = a[0:C]
        for hh in range(1, RWKV_GROUP):
            out = out + a[hh * C:(hh + 1) * C]
        return out

    nseq = r_ref.shape[0]
    groups = range(nseq * ngroups)
    unit = [(gi // ngroups, gi % ngroups) for gi in groups]
    sls = [(s, slice(None), slice(g * gw, (g + 1) * gw)) for s, g in unit]
    st = [state[s, g] for s, g in unit]
    lw = [lw_ref[sl] for sl in sls]
    r = [r_ref[sl].astype(F32) for sl in sls]
    k2 = [k2_ref[sl].astype(F32) for sl in sls]
    v = [v_ref[sl] for sl in sls]
    kk = [kk_ref[sl].astype(F32) for sl in sls]
    b = [b_ref[sl].astype(F32) for sl in sls]
    G = []
    for gi in groups:
        lhi, llo = _split(lw[gi])
        G.append(_dot(tri, lhi) + _dot(tri, llo))
    gc = [g[C - 1:C, :] for g in G]
    e_ng = [jnp.exp(-g) for g in G]
    e_c = [jnp.exp(gc[gi] - G[gi]) for gi in groups]
    rt = [(r[gi] * jnp.exp(G[gi])).astype(BF16) for gi in groups]
    kt = [(kk[gi] * jnp.exp(G[gi] - lw[gi])).astype(BF16) for gi in groups]
    kt_e = [expand(a) for a in kt]
    rt_e = [expand(a) for a in rt]
    kh_e = [expand((k2[gi] * e_ng[gi]).astype(BF16)) for gi in groups]
    bh_e = [expand((b[gi] * e_ng[gi]).astype(BF16)) for gi in groups]
    v_e = [expand(a) for a in v]
    p = [jnp.where(strict, -_dot_nt(kt_e[gi], bh_e[gi]), 0.0).astype(BF16) for gi in groups]
    m_kk = [jnp.where(strict, _dot_nt(kt_e[gi], kh_e[gi]), 0.0) for gi in groups]
    m_rk = [jnp.where(incl, _dot_nt(rt_e[gi], kh_e[gi]), 0.0) for gi in groups]
    m_rb = [jnp.where(incl, _dot_nt(rt_e[gi], bh_e[gi]), 0.0) for gi in groups]
    u = [_dot(m_kk[gi], v_e[gi]) + expand(_dot_nt(kt[gi], st[gi])) for gi in groups]
    for it in range(6):
        u = [u[gi] + _dot(p[gi], u[gi]) for gi in groups]
        if it < 5:
            p = [_dot(p[gi], p[gi]).astype(BF16) for gi in groups]
    ys = [collapse(_dot(m_rk[gi], v_e[gi]) - _dot(m_rb[gi], u[gi])) + _dot_nt(rt[gi], st[gi]) for gi in groups]
    for gi in groups:
        un = collapse(u[gi])
        upd = _dot_tn(v[gi], k2[gi] * e_c[gi]) - _dot_tn(un, b[gi] * e_c[gi])
        state[unit[gi]] = st[gi] * jnp.exp(gc[gi]) + jnp.where(same, upd, 0.0)
    y = jnp.concatenate([jnp.concatenate(ys[s * ngroups:(s + 1) * ngroups], axis=1) for s in range(nseq)], axis=0)
    sel = sel_ref[...]
    sel_t = selt_ref[...]
    mu = _head_sum(y, sel, sel_t) * (1.0 / n)
    yc = y - mu
    var = _head_sum(yc * yc, sel, sel_t) * (1.0 / n)
    gn = gn_ref[...]
    ygn = yc * lax.rsqrt(var + RWKV_GN_EPS) * gn[0:1] + gn[1:2]
    bonus = bonus_ref[...].reshape(nseq * C, D_MODEL).astype(F32)
    gate = g_ref[...].reshape(nseq * C, D_MODEL).astype(F32)
    o_ref[...] = ((ygn + bonus) * gate).astype(BF16).reshape(nseq, C, D_MODEL)


def _rwkv_mixer(x2, mod_l, bsz, seq, mu, w_in, w0, w_w1, w_w2, a0, w_a1, w_a2, w_g1, w_g2, k_k, k_a, r_k,
                gn_g, gn_b):
    ntok = x2.shape[0]
    tiles = ntok // ROW_TILE
    tps = seq // ROW_TILE
    d = D_MODEL
    wr, wk, wv = (w_in[j].astype(BF16) for j in range(3))
    ww1 = _pad_to(w_w1, 1, LANES).astype(BF16)
    ww2 = _pad_to(w_w2, 0, LANES).astype(BF16)
    wa1 = _pad_to(w_a1, 1, LANES).astype(BF16)
    wa2 = _pad_to(w_a2, 0, LANES).astype(BF16)
    wg1 = _pad_to(w_g1, 1, 2 * LANES).astype(BF16)
    wg2 = _pad_to(w_g2, 0, 2 * LANES).astype(BF16)
    vec = _pad_to(jnp.stack([w0, a0, k_k, k_a, r_k.reshape(d)]).astype(F32), 0, 8)
    sel = ((jnp.arange(d) // RWKV_HEAD_DIM)[:, None] == jnp.arange(LANES)[None, :]).astype(BF16)
    sel_t = jnp.transpose(sel)
    outs = pl.pallas_call(
        functools.partial(_rwkv_in_kernel, tiles_per_seq=tps),
        grid=(tiles,),
        in_specs=[_row_spec(d), _mod_spec(tps), _full_spec((6, d)), _full_spec((d, d)), _full_spec((d, d)),
                  _full_spec((d, d)), _full_spec(ww1.shape), _full_spec(ww2.shape), _full_spec(wa1.shape),
                  _full_spec(wa2.shape), _full_spec(wg1.shape), _full_spec(wg2.shape), _full_spec((8, d)),
                  _full_spec((d, LANES)), _full_spec((LANES, d))],
        out_specs=[_row_spec(d)] * 8,
        out_shape=[jax.ShapeDtypeStruct((ntok, d), F32 if j == 1 else BF16) for j in range(8)],
        scratch_shapes=[pltpu.VMEM((ROW_TILE + 8, d), F32)],
        compiler_params=_cparams("arbitrary"),
        name="rwkv_in",
    )(x2, mod_l, mu.astype(F32), wr, wk, wv, ww1, ww2, wa1, wa2, wg1, wg2, vec, sel, sel_t)
    r, lw, k2, v, kk, b, g, bonus = outs

    C = RWKV_CHUNK
    nc = seq // C
    nseq = RWKV_SEQS if bsz % RWKV_SEQS == 0 else 1
    chunk = pl.BlockSpec((nseq, C, d), lambda bb, c: (bb, c, 0))
    full2 = lambda shape: pl.BlockSpec(shape, lambda bb, c: (0, 0))
    gn = _pad_to(jnp.stack([gn_g, gn_b]).astype(F32), 0, 8)
    tri = jnp.tril(jnp.ones((C, C), BF16))
    gw = RWKV_GROUP * RWKV_HEAD_DIM
    by_seq = lambda a: a.reshape(bsz, seq, d)
    return pl.pallas_call(
        _rwkv_scan_kernel,
        grid=(bsz // nseq, nc),
        in_specs=[chunk] * 8 + [full2((8, d)), full2((C, C)), full2((d, LANES)), full2((LANES, d))],
        out_specs=chunk,
        out_shape=jax.ShapeDtypeStruct((bsz, seq, d), BF16),
        scratch_shapes=[pltpu.VMEM((nseq, d // gw, gw, gw), F32)],
        compiler_params=_cparams("parallel", "arbitrary"),
        name="rwkv_scan",
    )(*(by_seq(a) for a in (r, lw, k2, v, kk, b, g, bonus)), gn, tri, sel, sel_t).reshape(ntok, d)


def _s5_in_kernel(x_ref, mod_ref, w_ref, u_ref):
    h = _modulate(x_ref[...], mod_ref[0], 0, 1)
    u_ref[...] = _dot(h, w_ref[...])


def _s5_scan_kernel(u_ref, krow_ref, wst_ref, wout_ref, lam_ref, y_ref, kbig, lhs_sc, e_sc, xp_sc, xstate):
    q = pl.program_id(1)
    bsz, seq_rows, _ = u_ref.shape
    L = S5_CHUNK
    nq = seq_rows // L
    rows = bsz * nq

    @pl.when(q == 0)
    def _():
        xstate[...] = jnp.zeros(xstate.shape, F32)
        zero = jnp.zeros((LANES, LANES), BF16)
        for t in range(L):
            for s in range(L):
                blk = krow_ref[0, :, (t - s) * LANES:(t - s + 1) * LANES] if t >= s else zero
                kbig[t // 2, s * LANES:(s + 1) * LANES, (t % 2) * LANES:(t % 2 + 1) * LANES] = blk

    for s in range(L):
        lhs_sc[:, s * LANES:(s + 1) * LANES] = u_ref[:, pl.ds(s, nq, stride=L), :].reshape(rows, LANES).astype(BF16)
    e = jnp.dot(lhs_sc[...], wst_ref[0], preferred_element_type=F32)
    nslab = e_sc.shape[0]
    for j in range(nslab):
        e_sc[j] = e[:, j * LANES:(j + 1) * LANES]
    lam = lam_ref[0]
    l0 = lam[0:1]
    l1 = lam[1:2]
    half = xstate.shape[1] // 2

    def step(n, xst):
        sel = pl.ds(n, bsz, stride=nq)
        for j in range(nslab):
            xp_sc[j, sel, :] = xst[:, j * LANES:(j + 1) * LANES]
        e_n = jnp.concatenate([e_sc[j, sel, :] for j in range(nslab)], axis=1)
        return xst * l0 + pltpu.roll(xst, half, axis=1) * l1 + e_n

    xstate[...] = lax.fori_loop(0, nq, step, xstate[...])
    xprev = jnp.concatenate([xp_sc[j] for j in range(nslab)], axis=1).astype(BF16)

    def out_pair(i, carry):
        y2 = (jnp.dot(lhs_sc[...], kbig[i], preferred_element_type=F32)
              + jnp.dot(xprev, wout_ref[0, i], preferred_element_type=F32))
        for k in range(2):
            y_ref[:, pl.ds(2 * i + k, nq, stride=L), :] = y2[:, k * LANES:(k + 1) * LANES].reshape(bsz, nq, LANES)
        return carry

    lax.fori_loop(0, L // 2, out_pair, 0)


def _s5_glu_kernel(y_ref, u_ref, d_ref, w_ref, o_ref):
    y = y_ref[...] + d_ref[...] * u_ref[...]
    ge = 0.5 * y * (1.0 + jnp.tanh(math.sqrt(2.0 / math.pi) * (y + 0.044715 * (y * y * y))))
    ab = _dot(ge, w_ref[...])
    o_ref[...] = (ab[:, :D_MODEL] * _sigmoid(ab[:, D_MODEL:])).astype(BF16)


def _s5_operators(a_re, a_im, log_dt, b_re, b_im, c_re, c_im):
    hp = lax.Precision.HIGHEST
    L = S5_CHUNK
    a_re, a_im = a_re.astype(F32), a_im.astype(F32)
    dt = jnp.exp(log_dt.astype(F32))[:, None]
    mag = jnp.exp(a_re * dt)
    ab_re, ab_im = mag * jnp.cos(a_im * dt), mag * jnp.sin(a_im * dt)
    den = jnp.square(a_re) + jnp.square(a_im)
    f_re = ((ab_re - 1.0) * a_re + ab_im * a_im) / den
    f_im = (ab_im * a_re - (ab_re - 1.0) * a_im) / den
    b_re, b_im = b_re.astype(F32), b_im.astype(F32)
    bb_re = f_re[..., None] * b_re - f_im[..., None] * b_im
    bb_im = f_re[..., None] * b_im + f_im[..., None] * b_re
    tau = jnp.arange(L + 1, dtype=F32)[:, None, None]
    pmag = jnp.exp(tau * (a_re * dt)[None])
    pw_re = pmag * jnp.cos(tau * (a_im * dt)[None])
    pw_im = pmag * jnp.sin(tau * (a_im * dt)[None])
    c_re, c_im = c_re.astype(F32), c_im.astype(F32)
    cl_re = c_re[None] * pw_re[:, :, None, :] - c_im[None] * pw_im[:, :, None, :]
    cl_im = c_re[None] * pw_im[:, :, None, :] + c_im[None] * pw_re[:, :, None, :]
    kern = (jnp.einsum('tgcp,gpd->tgcd', cl_re, bb_re, precision=hp)
            - jnp.einsum('tgcp,gpd->tgcd', cl_im, bb_im, precision=hp))
    pk = S5_PACK
    npk = S5_N_GROUPS // pk
    eye = jnp.eye(pk, dtype=F32)
    krow = jnp.einsum('tGacd,ab->Gadtbc', kern[:L].reshape(L, npk, pk, S5_GROUP, S5_GROUP), eye)
    krow = krow.reshape(npk, pk * S5_GROUP, L * pk * S5_GROUP)
    rev_re = pw_re[L - 1 - jnp.arange(L)]
    rev_im = pw_im[L - 1 - jnp.arange(L)]
    st_re = rev_re[:, :, :, None] * bb_re[None] - rev_im[:, :, :, None] * bb_im[None]
    st_im = rev_re[:, :, :, None] * bb_im[None] + rev_im[:, :, :, None] * bb_re[None]
    st = jnp.stack([st_re, st_im]).reshape(2, L, npk, pk, S5_STATE, S5_GROUP)
    wst = jnp.einsum('qsGapd,ab->Gsadqbp', st, eye).reshape(npk, L * pk * S5_GROUP, 2 * pk * S5_STATE)
    cl = jnp.stack([cl_re[1:], -cl_im[1:]]).reshape(2, L, npk, pk, S5_GROUP, S5_STATE)
    wout = jnp.einsum('qtGacp,ab->Gqaptbc', cl, eye).reshape(npk, 2 * pk * S5_STATE, L // 2, 2 * pk * S5_GROUP)
    wout = jnp.transpose(wout, (0, 2, 1, 3))
    lr = pw_re[L].reshape(npk, pk * S5_STATE)
    li = pw_im[L].reshape(npk, pk * S5_STATE)
    lam = jnp.stack([jnp.concatenate([lr, lr], -1), jnp.concatenate([-li, li], -1)], axis=1)
    return krow.astype(BF16), wst.astype(BF16), wout.astype(BF16), lam


def _s5_mixer(x2, mod_l, bsz, seq, w_in, a_re, a_im, log_dt, b_re, b_im, c_re, c_im, d_skip, w_glu):
    ntok = x2.shape[0]
    tiles = ntok // ROW_TILE
    tps = seq // ROW_TILE
    d = D_MODEL
    u = pl.pallas_call(
        _s5_in_kernel,
        grid=(tiles,),
        in_specs=[_row_spec(d), _mod_spec(tps), _full_spec((d, d))],
        out_specs=_row_spec(d),
        out_shape=jax.ShapeDtypeStruct((ntok, d), F32),
        compiler_params=_cparams("parallel"),
        name="s5_in",
    )(x2, mod_l, w_in.astype(BF16))

    L = S5_CHUNK
    npk = S5_N_GROUPS // S5_PACK
    seq_rows = min(S5_STEP_CHUNKS * L, seq)
    nq = seq_rows // L
    width = L * LANES
    nstate = 2 * S5_PACK * S5_STATE
    krow, wst, wout, lam = _s5_operators(a_re, a_im, log_dt, b_re, b_im, c_re, c_im)
    blk = pl.BlockSpec((bsz, seq_rows, LANES), lambda g, q: (0, q, g))
    pack = lambda *shape: pl.BlockSpec((1,) + shape, lambda g, q: (g,) + (0,) * len(shape))
    y = pl.pallas_call(
        _s5_scan_kernel,
        grid=(npk, seq // seq_rows),
        in_specs=[blk, pack(LANES, width), pack(width, nstate), pack(L // 2, nstate, 2 * LANES), pack(2, nstate)],
        out_specs=blk,
        out_shape=jax.ShapeDtypeStruct((bsz, seq, d), F32),
        scratch_shapes=[pltpu.VMEM((L // 2, width, 2 * LANES), BF16), pltpu.VMEM((bsz * nq, width), BF16),
                        pltpu.VMEM((nstate // LANES, bsz * nq, LANES), F32),
                        pltpu.VMEM((nstate // LANES, bsz * nq, LANES), F32), pltpu.VMEM((bsz, nstate), F32)],
        compiler_params=_cparams("parallel", "arbitrary"),
        name="s5_scan",
    )(u.reshape(bsz, seq, d), krow, wst, wout, lam).reshape(ntok, d)

    return pl.pallas_call(
        _s5_glu_kernel,
        grid=(tiles,),
        in_specs=[_row_spec(d), _row_spec(d), _full_spec((1, d)), _full_spec((d, 2 * d))],
        out_specs=_row_spec(d),
        out_shape=jax.ShapeDtypeStruct((ntok, d), BF16),
        compiler_params=_cparams("parallel"),
        name="s5_glu",
    )(y, u, d_skip.astype(F32).reshape(1, d), w_glu.astype(BF16))


def _post_kernel(x_ref, y_ref, w_ref, mod_ref, ln_ref, rw_ref, rb_ref, tris_ref,
                 xn_ref, h2_ref, rout_ref, cnt_ref, run):
    i = pl.program_id(0)
    tm = x_ref.shape[0]

    @pl.when(i == 0)
    def _():
        run[...] = jnp.zeros(run.shape, F32)

    m = mod_ref[0]
    ln = ln_ref[...]
    y = jnp.dot(y_ref[...], w_ref[...], preferred_element_type=F32)
    xn = _layer_norm(DEEPNORM_ALPHA * x_ref[...] + (1.0 + m[2:3]) * y, ln[0:1], ln[1:2])
    xn_ref[...] = xn
    h2 = _modulate(xn, m, 3, 4)
    _store_rows(h2_ref, h2)

    hh, hl = _split(h2)
    logits = _dot_nt(rw_ref[0], hh) + _dot_nt(rw_ref[0], hl) + _dot_nt(rw_ref[1], hh)
    ex = jnp.exp(logits - jnp.max(logits, axis=0, keepdims=True))
    probs = ex / jnp.sum(ex, axis=0, keepdims=True)
    sel = probs + rb_ref[...]

    def first_max(vals):
        best = vals[0]
        for v in vals[1:]:
            best = jnp.maximum(best, v)
        idx = jnp.full(best.shape, len(vals) - 1, jnp.int32)
        for j in range(len(vals) - 2, -1, -1):
            idx = jnp.where(vals[j] == best, j, idx)
        return best, idx

    scores, firsts, seconds = [], [], []
    for g in range(N_EXPERT_GROUPS):
        s = [sel[g * EXPERTS_PER_GROUP + j:g * EXPERTS_PER_GROUP + j + 1, :] for j in range(EXPERTS_PER_GROUP)]
        v1, i1 = first_max(s)
        v2, i2 = first_max([jnp.where(i1 == j, NEG_BIG, s[j]) for j in range(EXPERTS_PER_GROUP)])
        scores.append(v1 + v2)
        firsts.append(i1)
        seconds.append(i2)
    _, grp = first_max(scores)
    e0 = jnp.zeros_like(grp)
    e1 = jnp.zeros_like(grp)
    for g in range(N_EXPERT_GROUPS):
        e0 = jnp.where(grp == g, g * EXPERTS_PER_GROUP + firsts[g], e0)
        e1 = jnp.where(grp == g, g * EXPERTS_PER_GROUP + seconds[g], e1)

    eidx = lax.broadcasted_iota(jnp.int32, (N_EXPERTS, tm), 0)
    hit0 = eidx == e0
    hit1 = eidx == e1
    w0 = jnp.sum(jnp.where(hit0, probs, 0.0), axis=0, keepdims=True)
    w1 = jnp.sum(jnp.where(hit1, probs, 0.0), axis=0, keepdims=True)
    wsum = w0 + w1
    onehot = hit0.astype(F32) + hit1.astype(F32)
    before = _dot(onehot, tris_ref[...]) + run[...]
    r0 = jnp.sum(jnp.where(hit0, before, 0.0), axis=0, keepdims=True)
    r1 = jnp.sum(jnp.where(hit1, before, 0.0), axis=0, keepdims=True)
    run[...] = run[...] + jnp.sum(onehot, axis=1, keepdims=True)
    cnt_ref[...] = jnp.broadcast_to(run[...], cnt_ref.shape)

    rid = lax.broadcasted_iota(jnp.int32, (8, tm), 0)
    rows = (e0.astype(F32), e1.astype(F32), w0 / wsum, w1 / wsum, r0, r1)
    out = jnp.zeros((8, tm), F32)
    for j, val in enumerate(rows):
        out = jnp.where(rid == j, val, out)
    rout_ref[...] = out


def _post(x2, y_in, w_out, mod_l, ln_g, ln_b, rw, rb, seq):
    ntok = x2.shape[0]
    tm = min(POST_TILE, seq)
    tiles = ntok // tm
    tps = seq // tm
    d = D_MODEL
    kin = y_in.shape[1]
    ln = _pad_to(jnp.stack([ln_g, ln_b]).astype(F32), 0, 8)
    tris = jnp.triu(jnp.ones((tm, tm), BF16), 1)
    return pl.pallas_call(
        _post_kernel,
        grid=(tiles,),
        in_specs=[_row_spec(d, tm), _row_spec(kin, tm), _full_spec((kin, d)), _mod_spec(tps), _full_spec((8, d)),
                  _full_spec((2, N_EXPERTS, d)), _full_spec((N_EXPERTS, 1)), _full_spec((tm, tm))],
        out_specs=[_row_spec(d, tm), _tile_rows_spec(tm), pl.BlockSpec((8, tm), lambda i: (0, i)),
                   _full_spec((N_EXPERTS, LANES))],
        out_shape=[jax.ShapeDtypeStruct((ntok, d), F32), jax.ShapeDtypeStruct((ntok, d // LANES, LANES), F32),
                   jax.ShapeDtypeStruct((8, ntok), F32), jax.ShapeDtypeStruct((N_EXPERTS, LANES), F32)],
        scratch_shapes=[pltpu.VMEM((N_EXPERTS, 1), F32)],
        compiler_params=_cparams("arbitrary"),
        name="post",
    )(x2, y_in, w_out.astype(BF16), mod_l, ln, rw, rb, tris)


def _ffn_kernel(be_ref, tok_ref, tokn_ref, h_ref, wu_ref, wd_ref, o_ref, xbuf, wub, wdb, sem):
    i = pl.program_id(0)
    last = pl.num_programs(0) - 1
    bm = o_ref.shape[0]
    slot = lax.rem(i, 2)
    rows = 128

    def gather(tok, slot_, t):
        return pltpu.make_async_copy(h_ref.at[pl.ds(tok[0, 0, t], 1)], xbuf.at[slot_, pl.ds(t, 1)], sem.at[slot_])

    @pl.when(i == 0)
    def _():
        for t in range(bm):
            gather(tok_ref, 0, t).start()

    prev = be_ref[jnp.maximum(i - 1, 0)]

    @pl.when((i == 0) | (be_ref[i] != prev))
    def _():
        def cast(j, carry):
            sl = pl.ds(pl.multiple_of(j * rows, rows), rows)
            wub[sl, :] = wu_ref[0, 0, sl, :].astype(BF16)
            wdb[sl, :] = wd_ref[0, 0, sl, :].astype(BF16)
            return carry
        lax.fori_loop(0, D_MODEL // rows, cast, 0)

    for t in range(bm):
        gather(tokn_ref, 1 - slot, t).start()
    for t in range(bm):
        gather(tok_ref, slot, t).wait()
    hid = jnp.dot(_load_rows(xbuf.at[slot]).astype(BF16), wub[...], preferred_element_type=F32)
    act = _silu(hid[:, :D_MODEL]) * hid[:, D_MODEL:]
    _store_rows(o_ref, jnp.dot(act.astype(BF16), wdb[...], preferred_element_type=F32))

    @pl.when(i == last)
    def _():
        for t in range(bm):
            gather(tokn_ref, 1 - slot, t).wait()


def _combine_kernel(dest_ref, destn_ref, xn_ref, wc_ref, mod_ref, ln_ref, ys_ref, o_ref, buf, sem):
    i = pl.program_id(0)
    last = pl.num_programs(0) - 1
    tm = xn_ref.shape[0]
    slot = lax.rem(i, 2)

    def gather(dref, slot_, k, t):
        return pltpu.make_async_copy(ys_ref.at[pl.ds(dref[0, 0, k * tm + t], 1)], buf.at[slot_, k, pl.ds(t, 1)],
                                     sem.at[slot_])

    def for_all(fn):
        for k in range(TOP_K):
            for t in range(tm):
                fn(k, t)

    @pl.when(i == 0)
    def _():
        for_all(lambda k, t: gather(dest_ref, 0, k, t).start())

    for_all(lambda k, t: gather(destn_ref, 1 - slot, k, t).start())
    for_all(lambda k, t: gather(dest_ref, slot, k, t).wait())
    m = mod_ref[0]
    ln = ln_ref[...]
    wc = wc_ref[...]
    y = wc[:, 0:1] * _load_rows(buf.at[slot, 0]) + wc[:, 1:2] * _load_rows(buf.at[slot, 1])
    o_ref[...] = _layer_norm(DEEPNORM_ALPHA * xn_ref[...] + (1.0 + m[5:6]) * y, ln[0:1], ln[1:2])

    @pl.when(i == last)
    def _():
        for_all(lambda k, t: gather(destn_ref, 1 - slot, k, t).wait())


def _moe(xn, h2, rout, cnt, mod_l, ln_g, ln_b, w_up, w_down, layer, seq):
    ntok = xn.shape[0]
    d = D_MODEL
    tiles = ntok // ROW_TILE
    tps = seq // ROW_TILE
    bm = MOE_ROWS
    n_assign = ntok * TOP_K
    n_blocks = n_assign // bm + N_EXPERTS
    n_pad = n_blocks * bm

    e0 = rout[0].astype(jnp.int32)
    e1 = rout[1].astype(jnp.int32)
    counts = cnt[:, 0].astype(jnp.int32)
    padded = (counts + bm - 1) // bm * bm
    pad_end = jnp.cumsum(padded)
    pad_start = pad_end - padded
    dest0 = pad_start[e0] + rout[4].astype(jnp.int32)
    dest1 = pad_start[e1] + rout[5].astype(jnp.int32)
    dest = jnp.concatenate([dest0.reshape(tiles, 1, ROW_TILE), dest1.reshape(tiles, 1, ROW_TILE)], axis=-1)
    tok = jnp.arange(ntok, dtype=jnp.int32)
    row_tok = jnp.zeros((n_pad,), jnp.int32).at[jnp.concatenate([dest0, dest1])].set(
        jnp.concatenate([tok, tok]), unique_indices=True).reshape(n_blocks, 1, bm)
    block_start = jnp.arange(n_blocks, dtype=jnp.int32) * bm
    block_expert = jnp.minimum(jnp.sum((block_start[:, None] >= pad_end[None, :]).astype(jnp.int32), axis=1),
                               N_EXPERTS - 1)
    wcol = jnp.transpose(rout[2:4])

    tok_spec = lambda nxt: pl.BlockSpec((1, 1, bm), lambda i, be: (jnp.minimum(i + nxt, n_blocks - 1), 0, 0),
                                        memory_space=pltpu.SMEM)
    ys = pl.pallas_call(
        _ffn_kernel,
        grid_spec=pltpu.PrefetchScalarGridSpec(
            num_scalar_prefetch=1,
            grid=(n_blocks,),
            in_specs=[tok_spec(0), tok_spec(1), pl.BlockSpec(memory_space=pl.ANY),
                      pl.BlockSpec((1, 1, d, 2 * d), lambda i, be: (layer, be[i], 0, 0)),
                      pl.BlockSpec((1, 1, d, d), lambda i, be: (layer, be[i], 0, 0))],
            out_specs=_tile_rows_spec(bm),
            scratch_shapes=[pltpu.VMEM((2, bm, d // LANES, LANES), F32), pltpu.VMEM((d, 2 * d), BF16),
                            pltpu.VMEM((d, d), BF16), pltpu.SemaphoreType.DMA((2,))],
        ),
        out_shape=jax.ShapeDtypeStruct((n_pad, d // LANES, LANES), F32),
        compiler_params=_cparams("arbitrary"),
        name="moe_ffn",
    )(block_expert, row_tok, row_tok, h2, w_up, w_down)

    ln = _pad_to(jnp.stack([ln_g, ln_b]).astype(F32), 0, 8)
    dest_spec = lambda nxt: pl.BlockSpec((1, 1, 2 * ROW_TILE), lambda i: (jnp.minimum(i + nxt, tiles - 1), 0, 0),
                                         memory_space=pltpu.SMEM)
    return pl.pallas_call(
        _combine_kernel,
        grid=(tiles,),
        in_specs=[dest_spec(0), dest_spec(1), _row_spec(d), pl.BlockSpec((ROW_TILE, 2), lambda i: (i, 0)),
                  _mod_spec(tps), _full_spec((8, d)), pl.BlockSpec(memory_space=pl.ANY)],
        out_specs=_row_spec(d),
        out_shape=jax.ShapeDtypeStruct((ntok, d), F32),
        scratch_shapes=[pltpu.VMEM((2, TOP_K, ROW_TILE, d // LANES, LANES), F32), pltpu.SemaphoreType.DMA((2,))],
        compiler_params=_cparams("arbitrary"),
        name="moe_combine",
    )(dest, dest, xn, wcol, mod_l, ln, ys)


def kernel(x, c, ada_w, ada_b, ln1_g, ln1_b, ln2_g, ln2_b, ssd_w_in, ssd_conv_w, ssd_conv_b, ssd_dt_bias, ssd_a_log, ssd_d, ssd_norm_g, ssd_w_out, gla_w_in, gla_w_gate, gla_b_gate, gla_norm_g, gla_w_out, rwkv_mu, rwkv_w_in, rwkv_w0, rwkv_w_w1, rwkv_w_w2, rwkv_a0, rwkv_w_a1, rwkv_w_a2, rwkv_w_g1, rwkv_w_g2, rwkv_k_k, rwkv_k_a, rwkv_r_k, rwkv_gn_g, rwkv_gn_b, rwkv_w_out, s5_w_in, s5_a_re, s5_a_im, s5_log_dt, s5_b_re, s5_b_im, s5_c_re, s5_c_im, s5_d, s5_w_glu, s5_w_out, moe_w_up, moe_w_down, router_w, router_b):
    bsz, seq, d = x.shape
    assert d == D_MODEL and seq % ROW_TILE == 0 and seq % SSD_CHUNK == 0
    depth = ada_w.shape[0]
    mod = _adaln_mod(c, ada_w, ada_b).reshape(depth, bsz, 6, d)
    rw = _hilo(jnp.transpose(router_w))
    rb = router_b.astype(F32).reshape(N_EXPERTS, 1)
    x2 = x.reshape(bsz * seq, d)
    for i in range(depth):
        kind, j = i % 4, i // 4
        mod_l = mod[i]
        if kind == 0:
            y = _ssd_mixer(x2, mod_l, bsz, seq, ssd_w_in[j], ssd_conv_w[j], ssd_conv_b[j], ssd_dt_bias[j],
                           ssd_a_log[j], ssd_d[j], ssd_norm_g[j])
            w_out = ssd_w_out[j]
        elif kind == 1:
            y = _gla_mixer(x2, mod_l, bsz, seq, gla_w_in[j], gla_w_gate[j], gla_b_gate[j], gla_norm_g[j])
            w_out = gla_w_out[j]
        elif kind == 2:
            y = _rwkv_mixer(x2, mod_l, bsz, seq, rwkv_mu[j], rwkv_w_in[j], rwkv_w0[j], rwkv_w_w1[j], rwkv_w_w2[j],
                            rwkv_a0[j], rwkv_w_a1[j], rwkv_w_a2[j], rwkv_w_g1[j], rwkv_w_g2[j], rwkv_k_k[j],
                            rwkv_k_a[j], rwkv_r_k[j], rwkv_gn_g[j], rwkv_gn_b[j])
            w_out = rwkv_w_out[j]
        else:
            y = _s5_mixer(x2, mod_l, bsz, seq, s5_w_in[j], s5_a_re[j], s5_a_im[j], s5_log_dt[j], s5_b_re[j],
                          s5_b_im[j], s5_c_re[j], s5_c_im[j], s5_d[j], s5_w_glu[j])
            w_out = s5_w_out[j]
        xn, h2, rout, cnt = _post(x2, y, w_out, mod_l, ln1_g[i], ln1_b[i], rw, rb, seq)
        x2 = _moe(xn, h2, rout, cnt, mod_l, ln2_g[i], ln2_b[i], moe_w_up, moe_w_down, i, seq)
    return x2.reshape(bsz, seq, d)
```

```python
import functools
import math

import jax
import jax.numpy as jnp
from jax import lax
from jax.experimental import pallas as pl
from jax.experimental.pallas import tpu as pltpu

F32 = jnp.float32
BF16 = jnp.bfloat16

D_MODEL = 1024
DEPTH = 4
DEEPNORM_ALPHA = (2 * DEPTH) ** 0.25
LN_EPS = 1e-5
RMS_EPS = 1e-5

SSD_D_INNER = 2048
SSD_HEAD_DIM = 64
SSD_N_HEADS = 32
SSD_N_GROUPS = 4
SSD_D_STATE = 128
SSD_CONV = 4
SSD_CONV_DIM = SSD_D_INNER + 2 * SSD_N_GROUPS * SSD_D_STATE
SSD_CHUNK = 128

GLA_N_HEADS = 4
GLA_D_K = 512
GLA_D_V = 1024
GLA_GATE_RANK = 16
GLA_TAU = 16.0
GLA_SUB = 16
GLA_ROWS = 256

RWKV_HEAD_DIM = 64
RWKV_N_HEADS = 16
RWKV_GN_EPS = 64e-5
RWKV_CHUNK = 64
RWKV_GROUP = 4
RWKV_SEQS = 2

S5_GROUP = 16
S5_N_GROUPS = 64
S5_STATE = 64
S5_CHUNK = 16
S5_PACK = 8
S5_STEP_CHUNKS = 16

N_EXPERTS = 16
N_EXPERT_GROUPS = 4
EXPERTS_PER_GROUP = 4
TOP_K = 2
MOE_ROWS = 256

ROW_TILE = 256
POST_TILE = 1024
LANES = 128
VMEM_LIMIT = 56 * 1024 * 1024
NEG_BIG = -1e30


def _cparams(*sem):
    return pltpu.CompilerParams(dimension_semantics=sem, vmem_limit_bytes=VMEM_LIMIT)


def _dot(a, b):
    return jnp.dot(a.astype(BF16), b.astype(BF16), preferred_element_type=F32)


def _dot_nt(a, b):
    return lax.dot_general(a.astype(BF16), b.astype(BF16), (((1,), (1,)), ((), ())), preferred_element_type=F32)


def _dot_tn(a, b):
    return lax.dot_general(a.astype(BF16), b.astype(BF16), (((0,), (0,)), ((), ())), preferred_element_type=F32)


def _split(x):
    hi = x.astype(BF16)
    lo = (x - hi.astype(F32)).astype(BF16)
    return hi, lo


def _split3(x):
    h1 = x.astype(BF16)
    r1 = x - h1.astype(F32)
    h2 = r1.astype(BF16)
    h3 = (r1 - h2.astype(F32)).astype(BF16)
    return h1, h2, h3


def _dot_x2(a, b_exact):
    hi, lo = _split(a)
    return _dot(hi, b_exact) + _dot(lo, b_exact)


def _dot_x3(a, b_exact):
    h1, h2, h3 = _split3(a)
    return _dot(h1, b_exact) + _dot(h2, b_exact) + _dot(h3, b_exact)


def _dot_hl(a_hi, a_lo, b_hi, b_lo):
    return _dot(a_hi, b_hi) + _dot(a_lo, b_hi) + _dot(a_hi, b_lo)


def _sigmoid(x):
    return 1.0 / (1.0 + jnp.exp(-x))


def _silu(x):
    return x * _sigmoid(x)


def _softplus(x):
    return jnp.maximum(x, 0.0) + jnp.log1p(jnp.exp(-jnp.abs(x)))


def _hilo(w):
    hi, lo = _split(w.astype(F32))
    return jnp.stack([hi, lo])


def _pad_to(a, axis, size):
    pad = [(0, 0)] * a.ndim
    pad[axis] = (0, size - a.shape[axis])
    return jnp.pad(a, pad)


def _modulate(x, m, shift_row, scale_row):
    return x * (1.0 + m[scale_row:scale_row + 1]) + m[shift_row:shift_row + 1]


def _layer_norm(v, g, b):
    mu = jnp.mean(v, axis=-1, keepdims=True)
    vc = v - mu
    var = jnp.mean(vc * vc, axis=-1, keepdims=True)
    return vc * lax.rsqrt(var + LN_EPS) * g + b


def _mod_kernel(c_ref, w_ref, b_ref, o_ref):
    o_ref[0] = _dot(_silu(c_ref[...]), w_ref[0]) + b_ref[0]


def _adaln_mod(c, ada_w, ada_b):
    nl, d, n6 = ada_w.shape
    bsz = c.shape[0]
    tn = 512
    return pl.pallas_call(
        _mod_kernel,
        grid=(nl, n6 // tn),
        in_specs=[
            pl.BlockSpec((bsz, d), lambda l, j: (0, 0)),
            pl.BlockSpec((1, d, tn), lambda l, j: (l, 0, j)),
            pl.BlockSpec((1, 1, tn), lambda l, j: (l, 0, j)),
        ],
        out_specs=pl.BlockSpec((1, bsz, tn), lambda l, j: (l, 0, j)),
        out_shape=jax.ShapeDtypeStruct((nl, bsz, n6), F32),
        compiler_params=_cparams("parallel", "parallel"),
        name="adaln_mod",
    )(c, ada_w, ada_b.reshape(nl, 1, n6))


def _row_spec(width, tm=ROW_TILE):
    return pl.BlockSpec((tm, width), lambda i: (i, 0))


def _full_spec(shape):
    nd = len(shape)
    return pl.BlockSpec(shape, lambda *_: (0,) * nd)


def _mod_spec(tiles_per_seq):
    return pl.BlockSpec((1, 6, D_MODEL), lambda i: (i // tiles_per_seq, 0, 0))


def _tile_rows_spec(tm):
    return pl.BlockSpec((tm, D_MODEL // LANES, LANES), lambda i, *_: (i, 0, 0))


def _store_rows(ref, val):
    for k in range(val.shape[1] // LANES):
        ref[:, k, :] = val[:, k * LANES:(k + 1) * LANES]


def _load_rows(ref):
    return jnp.concatenate([ref[:, k, :] for k in range(ref.shape[1])], axis=1)


def _ssd_in_kernel(x_ref, mod_ref, wz_ref, wx_ref, wdt_ref, z_ref, xbc_ref, dt_ref):
    h = _modulate(x_ref[...], mod_ref[0], 0, 1)
    hb, hl = _split(h)
    z_ref[...] = jnp.dot(hb, wz_ref[...], preferred_element_type=F32).astype(BF16)
    xbc_ref[...] = jnp.dot(hb, wx_ref[...], preferred_element_type=F32)
    dt_ref[...] = _dot_hl(hb, hl, wdt_ref[0], wdt_ref[1])


def _ssd_scan_kernel(z_ref, xbc_ref, dt_ref, cw_ref, cb_ref, dtb_ref, an_ref, dsk_ref, ng_ref,
                     e_ref, tri_ref, triu_ref, o_ref, xpad, state):
    c = pl.program_id(1)
    L = z_ref.shape[0]
    nstate = SSD_D_STATE

    @pl.when(c == 0)
    def _():
        state[...] = jnp.zeros(state.shape, F32)
        xpad[0:8, :] = jnp.zeros((8, SSD_CONV_DIM), F32)

    @pl.when(c != 0)
    def _():
        xpad[0:8, :] = xpad[L:L + 8, :]

    xpad[8:8 + L, :] = xbc_ref[...]

    acc = cb_ref[...] + cw_ref[3:4, :] * xpad[8:8 + L, :]
    for j in range(1, SSD_CONV):
        acc = acc + cw_ref[3 - j:4 - j, :] * xpad[pl.ds(8 - j, L), :]
    act = _silu(acc)
    xs = act[:, :SSD_D_INNER]
    bm = act[:, SSD_D_INNER:SSD_D_INNER + SSD_N_GROUPS * nstate]
    cm = act[:, SSD_D_INNER + SSD_N_GROUPS * nstate:]

    dtv = _softplus(dt_ref[...] + dtb_ref[...])
    dta = dtv * an_ref[...]
    dhi, dlo = _split(dta)
    tri = tri_ref[...]
    triu = triu_ref[...]
    acum = _dot(tri, dhi) + _dot(tri, dlo)
    acum_t = _dot_tn(dhi, triu) + _dot_tn(dlo, triu)

    e = e_ref[...]
    acum_x = _dot_x3(acum, e)
    dt_x = _dot_x3(dtv, e)
    eac_x = jnp.exp(acum_x)
    last_x = acum_x[L - 1:L, :]
    elast_x = jnp.exp(last_x)
    xdt = xs * dt_x
    xw = xs * (jnp.exp(last_x - acum_x) * dt_x)

    row = lax.broadcasted_iota(jnp.int32, (L, L), 0)
    col = lax.broadcasted_iota(jnp.int32, (L, L), 1)
    causal = row >= col
    lane = lax.broadcasted_iota(jnp.int32, (L, LANES), 1)
    left = lane < SSD_HEAD_DIM

    pieces = []
    heads_per_group = SSD_N_HEADS // SSD_N_GROUPS
    for g in range(SSD_N_GROUPS):
        bg = bm[:, g * nstate:(g + 1) * nstate]
        cg = cm[:, g * nstate:(g + 1) * nstate]
        cb = _dot_nt(cg, bg)
        for p in range(heads_per_group // 2):
            h0 = g * heads_per_group + 2 * p
            pair = h0 // 2
            sl = slice(h0 * SSD_HEAD_DIM, h0 * SSD_HEAD_DIM + LANES)
            ms = []
            for hh in (h0, h0 + 1):
                seg = acum[:, hh:hh + 1] - acum_t[hh:hh + 1, :]
                ms.append((cb * jnp.exp(jnp.where(causal, seg, NEG_BIG))).astype(BF16))
            mcat = jnp.concatenate(ms, axis=1)
            xp = xdt[:, sl]
            xbd = jnp.concatenate([jnp.where(left, xp, 0.0), jnp.where(left, 0.0, xp)], axis=0)
            st = state[pair]
            y_pair = _dot(mcat, xbd) + _dot(cg, st) * eac_x[:, sl]
            state[pair] = st * elast_x[:, sl] + _dot_tn(bg, xw[:, sl])
            pieces.append(y_pair)
    y = jnp.concatenate(pieces, axis=1) + dsk_ref[...] * xs
    y = y * _silu(z_ref[...].astype(F32))
    gw = SSD_D_INNER // SSD_N_GROUPS
    outs = []
    for g in range(SSD_N_GROUPS):
        yg = y[:, g * gw:(g + 1) * gw]
        outs.append(yg * lax.rsqrt(jnp.mean(yg * yg, axis=-1, keepdims=True) + RMS_EPS))
    o_ref[...] = (jnp.concatenate(outs, axis=1) * ng_ref[...]).astype(BF16)


def _ssd_mixer(x2, mod_l, bsz, seq, w_in, conv_w, conv_b, dt_bias, a_log, d_skip, norm_g):
    ntok = x2.shape[0]
    tiles = ntok // ROW_TILE
    tps = seq // ROW_TILE
    wz = w_in[:, :SSD_D_INNER].astype(BF16)
    wx = w_in[:, SSD_D_INNER:SSD_D_INNER + SSD_CONV_DIM].astype(BF16)
    wdt = _hilo(_pad_to(w_in[:, SSD_D_INNER + SSD_CONV_DIM:], 1, LANES))
    z, xbc, dt = pl.pallas_call(
        _ssd_in_kernel,
        grid=(tiles,),
        in_specs=[_row_spec(D_MODEL), _mod_spec(tps), _full_spec(wz.shape), _full_spec(wx.shape),
                  _full_spec(wdt.shape)],
        out_specs=[_row_spec(SSD_D_INNER), _row_spec(SSD_CONV_DIM), _row_spec(LANES)],
        out_shape=[jax.ShapeDtypeStruct((ntok, SSD_D_INNER), BF16),
                   jax.ShapeDtypeStruct((ntok, SSD_CONV_DIM), F32),
                   jax.ShapeDtypeStruct((ntok, LANES), F32)],
        compiler_params=_cparams("parallel"),
        name="ssd_in",
    )(x2, mod_l, wz, wx, wdt)

    L = SSD_CHUNK
    nc = seq // L
    a_neg = _pad_to((-jnp.exp(a_log.astype(F32))).reshape(1, SSD_N_HEADS), 1, LANES)
    dtb = _pad_to(dt_bias.astype(F32).reshape(1, SSD_N_HEADS), 1, LANES)
    dsk = jnp.repeat(d_skip.astype(F32), SSD_HEAD_DIM).reshape(1, SSD_D_INNER)
    expand = (jnp.arange(LANES)[:, None] == (jnp.arange(SSD_D_INNER) // SSD_HEAD_DIM)[None, :]).astype(BF16)
    tri = jnp.tril(jnp.ones((L, L), BF16))
    triu = jnp.triu(jnp.ones((L, L), BF16))
    chunk = lambda w: pl.BlockSpec((L, w), lambda b, c: (b * nc + c, 0))
    full2 = lambda shape: pl.BlockSpec(shape, lambda b, c: (0, 0))
    return pl.pallas_call(
        _ssd_scan_kernel,
        grid=(bsz, nc),
        in_specs=[chunk(SSD_D_INNER), chunk(SSD_CONV_DIM), chunk(LANES),
                  full2((SSD_CONV, SSD_CONV_DIM)), full2((1, SSD_CONV_DIM)), full2((1, LANES)), full2((1, LANES)),
                  full2((1, SSD_D_INNER)), full2((1, SSD_D_INNER)), full2((LANES, SSD_D_INNER)),
                  full2((L, L)), full2((L, L))],
        out_specs=chunk(SSD_D_INNER),
        out_shape=jax.ShapeDtypeStruct((ntok, SSD_D_INNER), BF16),
        scratch_shapes=[pltpu.VMEM((L + 8, SSD_CONV_DIM), F32),
                        pltpu.VMEM((SSD_N_HEADS // 2, SSD_D_STATE, LANES), F32)],
        compiler_params=_cparams("parallel", "arbitrary"),
        name="ssd_scan",
    )(z, xbc, dt, conv_w.astype(F32), conv_b.astype(F32).reshape(1, -1), dtb, a_neg, dsk,
      norm_g.astype(F32).reshape(1, -1), expand, tri, triu)


def _gla_in_kernel(x_ref, mod_ref, wq_ref, wk_ref, wv_ref, wr_ref, wg_ref, wgate_ref, bgate_ref,
                   q_ref, k_ref, v_ref, r_ref, la_ref):
    h = _modulate(x_ref[...], mod_ref[0], 0, 1)
    hb, hl = _split(h)
    dk = GLA_D_K // GLA_N_HEADS
    q_ref[...] = jnp.dot(hb, wq_ref[...], preferred_element_type=F32) * (dk ** -0.5)
    k_ref[...] = jnp.dot(hb, wk_ref[...], preferred_element_type=F32)
    v_ref[...] = jnp.dot(hb, wv_ref[...], preferred_element_type=F32)
    r_ref[...] = jnp.dot(hb, wr_ref[...], preferred_element_type=F32).astype(BF16)
    g_low = _dot_hl(hb, hl, wg_ref[0], wg_ref[1])
    gh, gl = _split(g_low)
    gz = _dot_hl(gh, gl, wgate_ref[0], wgate_ref[1]) + bgate_ref[...]
    la_ref[...] = (jnp.minimum(gz, 0.0) - jnp.log1p(jnp.exp(-jnp.abs(gz)))) * (1.0 / GLA_TAU)


def _gla_scan_kernel(q_ref, k_ref, v_ref, r_ref, la_ref, ng_ref, tri_ref, o_ref, state):
    c = pl.program_id(1)
    rows = q_ref.shape[0]
    sub = GLA_SUB
    nsub = rows // sub
    dk = GLA_D_K // GLA_N_HEADS
    dv = GLA_D_V // GLA_N_HEADS

    @pl.when(c == 0)
    def _():
        state[...] = jnp.zeros(state.shape, F32)

    q = q_ref[...]
    k = k_ref[...]
    v = v_ref[...]
    lhi, llo = _split(la_ref[...])
    tri = tri_ref[...]
    bcum = _dot(tri, lhi) + _dot(tri, llo)

    def sub_row(a, j):
        a3 = a.reshape(nsub, sub, a.shape[-1])
        return jnp.broadcast_to(a3[:, j:j + 1, :], a3.shape).reshape(a.shape)

    last = sub_row(bcum, sub - 1)
    qdec = q * jnp.exp(bcum)
    kdec = k * jnp.exp(last - bcum)
    elast = jnp.exp(last)

    lane = lax.broadcasted_iota(jnp.int32, (rows, LANES), 1)
    local_row = lax.broadcasted_iota(jnp.int32, (rows, LANES), 0) & (sub - 1)
    att = jnp.zeros((rows, LANES), F32)
    for h in range(GLA_N_HEADS):
        ks = slice(h * dk, (h + 1) * dk)
        qh, kh, bh = q[:, ks], k[:, ks], bcum[:, ks]
        for j in range(sub):
            ex = jnp.exp(jnp.minimum(bh - sub_row(bh, j), 0.0))
            a = jnp.sum(qh * sub_row(kh, j) * ex, axis=-1, keepdims=True)
            att = jnp.where(lane == h * sub + j, a, att)
    att = jnp.where((lane & (sub - 1)) <= local_row, att, 0.0)
    ahi, alo = _split(att)

    ng = ng_ref[...]
    value_head = lax.broadcasted_iota(jnp.int32, (sub, GLA_D_V), 1) >> (dv.bit_length() - 1)
    for s in range(nsub):
        rs = slice(s * sub, (s + 1) * sub)
        vs = v[rs]
        vbd = jnp.concatenate([jnp.where(value_head == h, vs, 0.0) for h in range(GLA_N_HEADS)], axis=0)
        used = GLA_N_HEADS * sub
        o_intra = _dot(ahi[rs, :used], vbd) + _dot(alo[rs, :used], vbd)
        outs = []
        for h in range(GLA_N_HEADS):
            ks = slice(h * dk, (h + 1) * dk)
            st = state[h]
            oh = o_intra[:, h * dv:(h + 1) * dv] + _dot_nt(qdec[rs, ks], st)
            state[h] = st * elast[s * sub:s * sub + 1, ks] + _dot_tn(vs[:, h * dv:(h + 1) * dv], kdec[rs, ks])
            outs.append(oh * lax.rsqrt(jnp.mean(oh * oh, axis=-1, keepdims=True) + RMS_EPS))
        on = jnp.concatenate(outs, axis=1) * ng
        o_ref[rs, :] = (on * _silu(r_ref[rs, :].astype(F32))).astype(BF16)


def _gla_mixer(x2, mod_l, bsz, seq, w_in, w_gate, b_gate, norm_g):
    ntok = x2.shape[0]
    tiles = ntok // ROW_TILE
    tps = seq // ROW_TILE
    o1, o2, o3 = GLA_D_K, 2 * GLA_D_K, 2 * GLA_D_K + GLA_D_V
    o4 = o3 + GLA_D_V
    wq, wk = w_in[:, :o1].astype(BF16), w_in[:, o1:o2].astype(BF16)
    wv, wr = w_in[:, o2:o3].astype(BF16), w_in[:, o3:o4].astype(BF16)
    wg = _hilo(_pad_to(w_in[:, o4:], 1, LANES))
    wgate = _hilo(_pad_to(w_gate, 0, LANES))
    q, k, v, r, la = pl.pallas_call(
        _gla_in_kernel,
        grid=(tiles,),
        in_specs=[_row_spec(D_MODEL), _mod_spec(tps), _full_spec(wq.shape), _full_spec(wk.shape),
                  _full_spec(wv.shape), _full_spec(wr.shape), _full_spec(wg.shape), _full_spec(wgate.shape),
                  _full_spec((1, GLA_D_K))],
        out_specs=[_row_spec(GLA_D_K), _row_spec(GLA_D_K), _row_spec(GLA_D_V), _row_spec(GLA_D_V),
                   _row_spec(GLA_D_K)],
        out_shape=[jax.ShapeDtypeStruct((ntok, GLA_D_K), F32), jax.ShapeDtypeStruct((ntok, GLA_D_K), F32),
                   jax.ShapeDtypeStruct((ntok, GLA_D_V), F32), jax.ShapeDtypeStruct((ntok, GLA_D_V), BF16),
                   jax.ShapeDtypeStruct((ntok, GLA_D_K), F32)],
        compiler_params=_cparams("parallel"),
        name="gla_in",
    )(x2, mod_l, wq, wk, wv, wr, wg, wgate, b_gate.astype(F32).reshape(1, -1))

    rows = GLA_ROWS
    nc = seq // rows
    chunk = lambda w: pl.BlockSpec((rows, w), lambda b, c: (b * nc + c, 0))
    full2 = lambda shape: pl.BlockSpec(shape, lambda b, c: (0, 0))
    idx = jnp.arange(rows)
    tri = ((idx[:, None] >= idx[None, :]) & (idx[:, None] // GLA_SUB == idx[None, :] // GLA_SUB)).astype(BF16)
    return pl.pallas_call(
        _gla_scan_kernel,
        grid=(bsz, nc),
        in_specs=[chunk(GLA_D_K), chunk(GLA_D_K), chunk(GLA_D_V), chunk(GLA_D_V), chunk(GLA_D_K),
                  full2((1, GLA_D_V)), full2((rows, rows))],
        out_specs=chunk(GLA_D_V),
        out_shape=jax.ShapeDtypeStruct((ntok, GLA_D_V), BF16),
        scratch_shapes=[pltpu.VMEM((GLA_N_HEADS, GLA_D_V // GLA_N_HEADS, GLA_D_K // GLA_N_HEADS), F32)],
        compiler_params=_cparams("parallel", "arbitrary"),
        name="gla_scan",
    )(q, k, v, r, la, norm_g.astype(F32).reshape(1, -1), tri)


def _head_sum(x, sel, sel_t):
    return _dot_x3(_dot_x2(x, sel), sel_t)


def _rwkv_in_kernel(x_ref, mod_ref, mu_ref, wr_ref, wk_ref, wv_ref, ww1_ref, ww2_ref, wa1_ref, wa2_ref,
                    wg1_ref, wg2_ref, vec_ref, sel_ref, selt_ref,
                    r_ref, lw_ref, k2_ref, v_ref, kk_ref, b_ref, g_ref, bonus_ref, hp, *, tiles_per_seq):
    i = pl.program_id(0)
    tm = x_ref.shape[0]
    h = _modulate(x_ref[...], mod_ref[0], 0, 1)

    @pl.when(i % tiles_per_seq == 0)
    def _():
        hp[0:8, :] = jnp.zeros((8, D_MODEL), F32)

    @pl.when(i % tiles_per_seq != 0)
    def _():
        hp[0:8, :] = hp[tm:tm + 8, :]

    hp[8:8 + tm, :] = h
    xx = hp[pl.ds(7, tm), :] - h
    mu = mu_ref[...]
    xr, xw, xk, xv, xa, xg = (h + xx * mu[j:j + 1] for j in range(6))
    vec = vec_ref[...]
    w0, a0, k_k, k_a, r_k = (vec[j:j + 1] for j in range(5))
    sel = sel_ref[...]
    sel_t = selt_ref[...]

    r = _dot(xr, wr_ref[...])
    k = _dot(xk, wk_ref[...])
    v = _dot(xv, wv_ref[...])
    wlog = -_softplus(-(w0 + _dot(jnp.tanh(_dot(xw, ww1_ref[...])), ww2_ref[...]))) - 0.5
    a = _sigmoid(a0 + _dot(_dot(xa, wa1_ref[...]), wa2_ref[...]))
    g = _dot(_sigmoid(_dot(xg, wg1_ref[...])), wg2_ref[...])
    kx = k * k_k
    norm = jnp.sqrt(_head_sum(kx * kx, sel, sel_t))
    kk = kx / jnp.maximum(norm, 1e-12)
    k2 = k * (1.0 + (a - 1.0) * k_a)
    bonus = _head_sum(r * k2 * r_k, sel, sel_t) * v

    r_ref[...] = r.astype(BF16)
    lw_ref[...] = -jnp.exp(wlog)
    k2_ref[...] = k2.astype(BF16)
    v_ref[...] = v.astype(BF16)
    kk_ref[...] = kk.astype(BF16)
    b_ref[...] = (kk * a).astype(BF16)
    g_ref[...] = g.astype(BF16)
    bonus_ref[...] = bonus.astype(BF16)


def _rwkv_scan_kernel(r_ref, lw_ref, k2_ref, v_ref, kk_ref, b_ref, g_ref, bonus_ref, gn_ref, tri_ref, sel_ref,
                      selt_ref, o_ref, state):
    c = pl.program_id(1)
    C = RWKV_CHUNK
    n = RWKV_HEAD_DIM
    gw = RWKV_GROUP * n
    ngroups = D_MODEL // gw

    @pl.when(c == 0)
    def _():
        state[...] = jnp.zeros(state.shape, F32)

    tri = tri_ref[...]
    ri = lax.broadcasted_iota(jnp.int32, (gw, gw), 0)
    ci = lax.broadcasted_iota(jnp.int32, (gw, gw), 1)
    shift = n.bit_length() - 1
    same = (ri >> shift) == (ci >> shift)
    strict = same & ((ri & (C - 1)) > (ci & (C - 1)))
    incl = same & ((ri & (C - 1)) >= (ci & (C - 1)))
    lane_head = lax.broadcasted_iota(jnp.int32, (C, gw), 1) >> shift

    def expand(a):
        return jnp.concatenate([jnp.where(lane_head == hh, a, 0.0) for hh in range(RWKV_GROUP)], axis=0)

    def collapse(a):
        out = a[0:C]
        for hh in range(1, RWKV_GROUP):
            out = out + a[hh * C:(hh + 1) * C]
        return out

    nseq = r_ref.shape[0]
    groups = range(nseq * ngroups)
    unit = [(gi // ngroups, gi % ngroups) for gi in groups]
    sls = [(s, slice(None), slice(g * gw, (g + 1) * gw)) for s, g in unit]
    st = [state[s, g] for s, g in unit]
    lw = [lw_ref[sl] for sl in sls]
    r = [r_ref[sl].astype(F32) for sl in sls]
    k2 = [k2_ref[sl].astype(F32) for sl in sls]
    v = [v_ref[sl] for sl in sls]
    kk = [kk_ref[sl].astype(F32) for sl in sls]
    b = [b_ref[sl].astype(F32) for sl in sls]
    G = []
    for gi in groups:
        lhi, llo = _split(lw[gi])
        G.append(_dot(tri, lhi) + _dot(tri, llo))
    gc = [g[C - 1:C, :] for g in G]
    e_ng = [jnp.exp(-g) for g in G]
    e_c = [jnp.exp(gc[gi] - G[gi]) for gi in groups]
    rt = [(r[gi] * jnp.exp(G[gi])).astype(BF16) for gi in groups]
    kt = [(kk[gi] * jnp.exp(G[gi] - lw[gi])).astype(BF16) for gi in groups]
    kt_e = [expand(a) for a in kt]
    rt_e = [expand(a) for a in rt]
    kh_e = [expand((k2[gi] * e_ng[gi]).astype(BF16)) for gi in groups]
    bh_e = [expand((b[gi] * e_ng[gi]).astype(BF16)) for gi in groups]
    v_e = [expand(a) for a in v]
    p = [jnp.where(strict, -_dot_nt(kt_e[gi], bh_e[gi]), 0.0).astype(BF16) for gi in groups]
    m_kk = [jnp.where(strict, _dot_nt(kt_e[gi], kh_e[gi]), 0.0) for gi in groups]
    m_rk = [jnp.where(incl, _dot_nt(rt_e[gi], kh_e[gi]), 0.0) for gi in groups]
    m_rb = [jnp.where(incl, _dot_nt(rt_e[gi], bh_e[gi]), 0.0) for gi in groups]
    u = [_dot(m_kk[gi], v_e[gi]) + expand(_dot_nt(kt[gi], st[gi])) for gi in groups]
    for it in range(6):
        u = [u[gi] + _dot(p[gi], u[gi]) for gi in groups]
        if it < 5:
            p = [_dot(p[gi], p[gi]).astype(BF16) for gi in groups]
    ys = [collapse(_dot(m_rk[gi], v_e[gi]) - _dot(m_rb[gi], u[gi])) + _dot_nt(rt[gi], st[gi]) for gi in groups]
    for gi in groups:
        un = collapse(u[gi])
        upd = _dot_tn(v[gi], k2[gi] * e_c[gi]) - _dot_tn(un, b[gi] * e_c[gi])
        state[unit[gi]] = st[gi] * jnp.exp(gc[gi]) + jnp.where(same, upd, 0.0)
    y = jnp.concatenate([jnp.concatenate(ys[s * ngroups:(s + 1) * ngroups], axis=1) for s in range(nseq)], axis=0)
    sel = sel_ref[...]
    sel_t = selt_ref[...]
    mu = _head_sum(y, sel, sel_t) * (1.0 / n)
    yc = y - mu
    var = _head_sum(yc * yc, sel, sel_t) * (1.0 / n)
    gn = gn_ref[...]
    ygn = yc * lax.rsqrt(var + RWKV_GN_EPS) * gn[0:1] + gn[1:2]
    bonus = bonus_ref[...].reshape(nseq * C, D_MODEL).astype(F32)
    gate = g_ref[...].reshape(nseq * C, D_MODEL).astype(F32)
    o_ref[...] = ((ygn + bonus) * gate).astype(BF16).reshape(nseq, C, D_MODEL)


def _rwkv_mixer(x2, mod_l, bsz, seq, mu, w_in, w0, w_w1, w_w2, a0, w_a1, w_a2, w_g1, w_g2, k_k, k_a, r_k,
                gn_g, gn_b):
    ntok = x2.shape[0]
    tiles = ntok // ROW_TILE
    tps = seq // ROW_TILE
    d = D_MODEL
    wr, wk, wv = (w_in[j].astype(BF16) for j in range(3))
    ww1 = _pad_to(w_w1, 1, LANES).astype(BF16)
    ww2 = _pad_to(w_w2, 0, LANES).astype(BF16)
    wa1 = _pad_to(w_a1, 1, LANES).astype(BF16)
    wa2 = _pad_to(w_a2, 0, LANES).astype(BF16)
    wg1 = _pad_to(w_g1, 1, 2 * LANES).astype(BF16)
    wg2 = _pad_to(w_g2, 0, 2 * LANES).astype(BF16)
    vec = _pad_to(jnp.stack([w0, a0, k_k, k_a, r_k.reshape(d)]).astype(F32), 0, 8)
    sel = ((jnp.arange(d) // RWKV_HEAD_DIM)[:, None] == jnp.arange(LANES)[None, :]).astype(BF16)
    sel_t = jnp.transpose(sel)
    outs = pl.pallas_call(
        functools.partial(_rwkv_in_kernel, tiles_per_seq=tps),
        grid=(tiles,),
        in_specs=[_row_spec(d), _mod_spec(tps), _full_spec((6, d)), _full_spec((d, d)), _full_spec((d, d)),
                  _full_spec((d, d)), _full_spec(ww1.shape), _full_spec(ww2.shape), _full_spec(wa1.shape),
                  _full_spec(wa2.shape), _full_spec(wg1.shape), _full_spec(wg2.shape), _full_spec((8, d)),
                  _full_spec((d, LANES)), _full_spec((LANES, d))],
        out_specs=[_row_spec(d)] * 8,
        out_shape=[jax.ShapeDtypeStruct((ntok, d), F32 if j == 1 else BF16) for j in range(8)],
        scratch_shapes=[pltpu.VMEM((ROW_TILE + 8, d), F32)],
        compiler_params=_cparams("arbitrary"),
        name="rwkv_in",
    )(x2, mod_l, mu.astype(F32), wr, wk, wv, ww1, ww2, wa1, wa2, wg1, wg2, vec, sel, sel_t)
    r, lw, k2, v, kk, b, g, bonus = outs

    C = RWKV_CHUNK
    nc = seq // C
    nseq = RWKV_SEQS if bsz % RWKV_SEQS == 0 else 1
    chunk = pl.BlockSpec((nseq, C, d), lambda bb, c: (bb, c, 0))
    full2 = lambda shape: pl.BlockSpec(shape, lambda bb, c: (0, 0))
    gn = _pad_to(jnp.stack([gn_g, gn_b]).astype(F32), 0, 8)
    tri = jnp.tril(jnp.ones((C, C), BF16))
    gw = RWKV_GROUP * RWKV_HEAD_DIM
    by_seq = lambda a: a.reshape(bsz, seq, d)
    return pl.pallas_call(
        _rwkv_scan_kernel,
        grid=(bsz // nseq, nc),
        in_specs=[chunk] * 8 + [full2((8, d)), full2((C, C)), full2((d, LANES)), full2((LANES, d))],
        out_specs=chunk,
        out_shape=jax.ShapeDtypeStruct((bsz, seq, d), BF16),
        scratch_shapes=[pltpu.VMEM((nseq, d // gw, gw, gw), F32)],
        compiler_params=_cparams("parallel", "arbitrary"),
        name="rwkv_scan",
    )(*(by_seq(a) for a in (r, lw, k2, v, kk, b, g, bonus)), gn, tri, sel, sel_t).reshape(ntok, d)


def _s5_in_kernel(x_ref, mod_ref, w_ref, u_ref):
    h = _modulate(x_ref[...], mod_ref[0], 0, 1)
    u_ref[...] = _dot(h, w_ref[...])


def _s5_scan_kernel(u_ref, krow_ref, wst_ref, wout_ref, lam_ref, y_ref, kbig, lhs_sc, e_sc, xp_sc, xstate):
    q = pl.program_id(1)
    bsz, seq_rows, _ = u_ref.shape
    L = S5_CHUNK
    nq = seq_rows // L
    rows = bsz * nq

    @pl.when(q == 0)
    def _():
        xstate[...] = jnp.zeros(xstate.shape, F32)
        zero = jnp.zeros((LANES, LANES), BF16)
        for t in range(L):
            for s in range(L):
                blk = krow_ref[0, :, (t - s) * LANES:(t - s + 1) * LANES] if t >= s else zero
                kbig[t // 2, s * LANES:(s + 1) * LANES, (t % 2) * LANES:(t % 2 + 1) * LANES] = blk

    for s in range(L):
        lhs_sc[:, s * LANES:(s + 1) * LANES] = u_ref[:, pl.ds(s, nq, stride=L), :].reshape(rows, LANES).astype(BF16)
    e = jnp.dot(lhs_sc[...], wst_ref[0], preferred_element_type=F32)
    nslab = e_sc.shape[0]
    for j in range(nslab):
        e_sc[j] = e[:, j * LANES:(j + 1) * LANES]
    lam = lam_ref[0]
    l0 = lam[0:1]
    l1 = lam[1:2]
    half = xstate.shape[1] // 2

    def step(n, xst):
        sel = pl.ds(n, bsz, stride=nq)
        for j in range(nslab):
            xp_sc[j, sel, :] = xst[:, j * LANES:(j + 1) * LANES]
        e_n = jnp.concatenate([e_sc[j, sel, :] for j in range(nslab)], axis=1)
        return xst * l0 + pltpu.roll(xst, half, axis=1) * l1 + e_n

    xstate[...] = lax.fori_loop(0, nq, step, xstate[...])
    xprev = jnp.concatenate([xp_sc[j] for j in range(nslab)], axis=1).astype(BF16)

    def out_pair(i, carry):
        y2 = (jnp.dot(lhs_sc[...], kbig[i], preferred_element_type=F32)
              + jnp.dot(xprev, wout_ref[0, i], preferred_element_type=F32))
        for k in range(2):
            y_ref[:, pl.ds(2 * i + k, nq, stride=L), :] = y2[:, k * LANES:(k + 1) * LANES].reshape(bsz, nq, LANES)
        return carry

    lax.fori_loop(0, L // 2, out_pair, 0)


def _s5_glu_kernel(y_ref, u_ref, d_ref, w_ref, o_ref):
    y = y_ref[...] + d_ref[...] * u_ref[...]
    ge = 0.5 * y * (1.0 + jnp.tanh(math.sqrt(2.0 / math.pi) * (y + 0.044715 * (y * y * y))))
    ab = _dot(ge, w_ref[...])
    o_ref[...] = (ab[:, :D_MODEL] * _sigmoid(ab[:, D_MODEL:])).astype(BF16)


def _s5_operators(a_re, a_im, log_dt, b_re, b_im, c_re, c_im):
    hp = lax.Precision.HIGHEST
    L = S5_CHUNK
    a_re, a_im = a_re.astype(F32), a_im.astype(F32)
    dt = jnp.exp(log_dt.astype(F32))[:, None]
    mag = jnp.exp(a_re * dt)
    ab_re, ab_im = mag * jnp.cos(a_im * dt), mag * jnp.sin(a_im * dt)
    den = jnp.square(a_re) + jnp.square(a_im)
    f_re = ((ab_re - 1.0) * a_re + ab_im * a_im) / den
    f_im = (ab_im * a_re - (ab_re - 1.0) * a_im) / den
    b_re, b_im = b_re.astype(F32), b_im.astype(F32)
    bb_re = f_re[..., None] * b_re - f_im[..., None] * b_im
    bb_im = f_re[..., None] * b_im + f_im[..., None] * b_re
    tau = jnp.arange(L + 1, dtype=F32)[:, None, None]
    pmag = jnp.exp(tau * (a_re * dt)[None])
    pw_re = pmag * jnp.cos(tau * (a_im * dt)[None])
    pw_im = pmag * jnp.sin(tau * (a_im * dt)[None])
    c_re, c_im = c_re.astype(F32), c_im.astype(F32)
    cl_re = c_re[None] * pw_re[:, :, None, :] - c_im[None] * pw_im[:, :, None, :]
    cl_im = c_re[None] * pw_im[:, :, None, :] + c_im[None] * pw_re[:, :, None, :]
    kern = (jnp.einsum('tgcp,gpd->tgcd', cl_re, bb_re, precision=hp)
            - jnp.einsum('tgcp,gpd->tgcd', cl_im, bb_im, precision=hp))
    pk = S5_PACK
    npk = S5_N_GROUPS // pk
    eye = jnp.eye(pk, dtype=F32)
    krow = jnp.einsum('tGacd,ab->Gadtbc', kern[:L].reshape(L, npk, pk, S5_GROUP, S5_GROUP), eye)
    krow = krow.reshape(npk, pk * S5_GROUP, L * pk * S5_GROUP)
    rev_re = pw_re[L - 1 - jnp.arange(L)]
    rev_im = pw_im[L - 1 - jnp.arange(L)]
    st_re = rev_re[:, :, :, None] * bb_re[None] - rev_im[:, :, :, None] * bb_im[None]
    st_im = rev_re[:, :, :, None] * bb_im[None] + rev_im[:, :, :, None] * bb_re[None]
    st = jnp.stack([st_re, st_im]).reshape(2, L, npk, pk, S5_STATE, S5_GROUP)
    wst = jnp.einsum('qsGapd,ab->Gsadqbp', st, eye).reshape(npk, L * pk * S5_GROUP, 2 * pk * S5_STATE)
    cl = jnp.stack([cl_re[1:], -cl_im[1:]]).reshape(2, L, npk, pk, S5_GROUP, S5_STATE)
    wout = jnp.einsum('qtGacp,ab->Gqaptbc', cl, eye).reshape(npk, 2 * pk * S5_STATE, L // 2, 2 * pk * S5_GROUP)
    wout = jnp.transpose(wout, (0, 2, 1, 3))
    lr = pw_re[L].reshape(npk, pk * S5_STATE)
    li = pw_im[L].reshape(npk, pk * S5_STATE)
    lam = jnp.stack([jnp.concatenate([lr, lr], -1), jnp.concatenate([-li, li], -1)], axis=1)
    return krow.astype(BF16), wst.astype(BF16), wout.astype(BF16), lam


def _s5_mixer(x2, mod_l, bsz, seq, w_in, a_re, a_im, log_dt, b_re, b_im, c_re, c_im, d_skip, w_glu):
    ntok = x2.shape[0]
    tiles = ntok // ROW_TILE
    tps = seq // ROW_TILE
    d = D_MODEL
    u = pl.pallas_call(
        _s5_in_kernel,
        grid=(tiles,),
        in_specs=[_row_spec(d), _mod_spec(tps), _full_spec((d, d))],
        out_specs=_row_spec(d),
        out_shape=jax.ShapeDtypeStruct((ntok, d), F32),
        compiler_params=_cparams("parallel"),
        name="s5_in",
    )(x2, mod_l, w_in.astype(BF16))

    L = S5_CHUNK
    npk = S5_N_GROUPS // S5_PACK
    seq_rows = min(S5_STEP_CHUNKS * L, seq)
    nq = seq_rows // L
    width = L * LANES
    nstate = 2 * S5_PACK * S5_STATE
    krow, wst, wout, lam = _s5_operators(a_re, a_im, log_dt, b_re, b_im, c_re, c_im)
    blk = pl.BlockSpec((bsz, seq_rows, LANES), lambda g, q: (0, q, g))
    pack = lambda *shape: pl.BlockSpec((1,) + shape, lambda g, q: (g,) + (0,) * len(shape))
    y = pl.pallas_call(
        _s5_scan_kernel,
        grid=(npk, seq // seq_rows),
        in_specs=[blk, pack(LANES, width), pack(width, nstate), pack(L // 2, nstate, 2 * LANES), pack(2, nstate)],
        out_specs=blk,
        out_shape=jax.ShapeDtypeStruct((bsz, seq, d), F32),
        scratch_shapes=[pltpu.VMEM((L // 2, width, 2 * LANES), BF16), pltpu.VMEM((bsz * nq, width), BF16),
                        pltpu.VMEM((nstate // LANES, bsz * nq, LANES), F32),
                        pltpu.VMEM((nstate // LANES, bsz * nq, LANES), F32), pltpu.VMEM((bsz, nstate), F32)],
        compiler_params=_cparams("parallel", "arbitrary"),
        name="s5_scan",
    )(u.reshape(bsz, seq, d), krow, wst, wout, lam).reshape(ntok, d)

    return pl.pallas_call(
        _s5_glu_kernel,
        grid=(tiles,),
        in_specs=[_row_spec(d), _row_spec(d), _full_spec((1, d)), _full_spec((d, 2 * d))],
        out_specs=_row_spec(d),
        out_shape=jax.ShapeDtypeStruct((ntok, d), BF16),
        compiler_params=_cparams("parallel"),
        name="s5_glu",
    )(y, u, d_skip.astype(F32).reshape(1, d), w_glu.astype(BF16))


def _post_kernel(x_ref, y_ref, w_ref, mod_ref, ln_ref, rw_ref, rb_ref, tris_ref,
                 xn_ref, h2_ref, rout_ref, cnt_ref, run):
    i = pl.program_id(0)
    tm = x_ref.shape[0]

    @pl.when(i == 0)
    def _():
        run[...] = jnp.zeros(run.shape, F32)

    m = mod_ref[0]
    ln = ln_ref[...]
    y = jnp.dot(y_ref[...], w_ref[...], preferred_element_type=F32)
    xn = _layer_norm(DEEPNORM_ALPHA * x_ref[...] + (1.0 + m[2:3]) * y, ln[0:1], ln[1:2])
    xn_ref[...] = xn
    h2 = _modulate(xn, m, 3, 4)
    _store_rows(h2_ref, h2)

    hh, hl = _split(h2)
    logits = _dot_nt(rw_ref[0], hh) + _dot_nt(rw_ref[0], hl) + _dot_nt(rw_ref[1], hh)
    ex = jnp.exp(logits - jnp.max(logits, axis=0, keepdims=True))
    probs = ex / jnp.sum(ex, axis=0, keepdims=True)
    sel = probs + rb_ref[...]

    def first_max(vals):
        best = vals[0]
        for v in vals[1:]:
            best = jnp.maximum(best, v)
        idx = jnp.full(best.shape, len(vals) - 1, jnp.int32)
        for j in range(len(vals) - 2, -1, -1):
            idx = jnp.where(vals[j] == best, j, idx)
        return best, idx

    scores, firsts, seconds = [], [], []
    for g in range(N_EXPERT_GROUPS):
        s = [sel[g * EXPERTS_PER_GROUP + j:g * EXPERTS_PER_GROUP + j + 1, :] for j in range(EXPERTS_PER_GROUP)]
        v1, i1 = first_max(s)
        v2, i2 = first_max([jnp.where(i1 == j, NEG_BIG, s[j]) for j in range(EXPERTS_PER_GROUP)])
        scores.append(v1 + v2)
        firsts.append(i1)
        seconds.append(i2)
    _, grp = first_max(scores)
    e0 = jnp.zeros_like(grp)
    e1 = jnp.zeros_like(grp)
    for g in range(N_EXPERT_GROUPS):
        e0 = jnp.where(grp == g, g * EXPERTS_PER_GROUP + firsts[g], e0)
        e1 = jnp.where(grp == g, g * EXPERTS_PER_GROUP + seconds[g], e1)

    eidx = lax.broadcasted_iota(jnp.int32, (N_EXPERTS, tm), 0)
    hit0 = eidx == e0
    hit1 = eidx == e1
    w0 = jnp.sum(jnp.where(hit0, probs, 0.0), axis=0, keepdims=True)
    w1 = jnp.sum(jnp.where(hit1, probs, 0.0), axis=0, keepdims=True)
    wsum = w0 + w1
    onehot = hit0.astype(F32) + hit1.astype(F32)
    before = _dot(onehot, tris_ref[...]) + run[...]
    r0 = jnp.sum(jnp.where(hit0, before, 0.0), axis=0, keepdims=True)
    r1 = jnp.sum(jnp.where(hit1, before, 0.0), axis=0, keepdims=True)
    run[...] = run[...] + jnp.sum(onehot, axis=1, keepdims=True)
    cnt_ref[...] = jnp.broadcast_to(run[...], cnt_ref.shape)

    rid = lax.broadcasted_iota(jnp.int32, (8, tm), 0)
    rows = (e0.astype(F32), e1.astype(F32), w0 / wsum, w1 / wsum, r0, r1)
    out = jnp.zeros((8, tm), F32)
    for j, val in enumerate(rows):
        out = jnp.where(rid == j, val, out)
    rout_ref[...] = out


def _post(x2, y_in, w_out, mod_l, ln_g, ln_b, rw, rb, seq):
    ntok = x2.shape[0]
    tm = min(POST_TILE, seq)
    tiles = ntok // tm
    tps = seq // tm
    d = D_MODEL
    kin = y_in.shape[1]
    ln = _pad_to(jnp.stack([ln_g, ln_b]).astype(F32), 0, 8)
    tris = jnp.triu(jnp.ones((tm, tm), BF16), 1)
    return pl.pallas_call(
        _post_kernel,
        grid=(tiles,),
        in_specs=[_row_spec(d, tm), _row_spec(kin, tm), _full_spec((kin, d)), _mod_spec(tps), _full_spec((8, d)),
                  _full_spec((2, N_EXPERTS, d)), _full_spec((N_EXPERTS, 1)), _full_spec((tm, tm))],
        out_specs=[_row_spec(d, tm), _tile_rows_spec(tm), pl.BlockSpec((8, tm), lambda i: (0, i)),
                   _full_spec((N_EXPERTS, LANES))],
        out_shape=[jax.ShapeDtypeStruct((ntok, d), F32), jax.ShapeDtypeStruct((ntok, d // LANES, LANES), F32),
                   jax.ShapeDtypeStruct((8, ntok), F32), jax.ShapeDtypeStruct((N_EXPERTS, LANES), F32)],
        scratch_shapes=[pltpu.VMEM((N_EXPERTS, 1), F32)],
        compiler_params=_cparams("arbitrary"),
        name="post",
    )(x2, y_in, w_out.astype(BF16), mod_l, ln, rw, rb, tris)


def _ffn_kernel(be_ref, tok_ref, tokn_ref, h_ref, wu_ref, wd_ref, o_ref, xbuf, wub, wdb, sem):
    i = pl.program_id(0)
    last = pl.num_programs(0) - 1
    bm = o_ref.shape[0]
    slot = lax.rem(i, 2)
    rows = 128

    def gather(tok, slot_, t):
        return pltpu.make_async_copy(h_ref.at[pl.ds(tok[0, 0, t], 1)], xbuf.at[slot_, pl.ds(t, 1)], sem.at[slot_])

    @pl.when(i == 0)
    def _():
        for t in range(bm):
            gather(tok_ref, 0, t).start(priority=t % 2)

    prev = be_ref[jnp.maximum(i - 1, 0)]

    @pl.when((i == 0) | (be_ref[i] != prev))
    def _():
        def cast(j, carry):
            sl = pl.ds(pl.multiple_of(j * rows, rows), rows)
            wub[sl, :] = wu_ref[0, 0, sl, :].astype(BF16)
            wdb[sl, :] = wd_ref[0, 0, sl, :].astype(BF16)
            return carry
        lax.fori_loop(0, D_MODEL // rows, cast, 0)

    for t in range(bm):
        gather(tok_ref, slot, t).wait()
    x = _load_rows(xbuf.at[slot]).astype(BF16)
    slabs = 8
    cols = 2 * D_MODEL // slabs
    per = bm // slabs
    parts = []
    for c in range(slabs):
        parts.append(jnp.dot(x, wub[:, c * cols:(c + 1) * cols], preferred_element_type=F32))
        for t in range(c * per, (c + 1) * per):
            gather(tokn_ref, 1 - slot, t).start(priority=t % 2)
    hid = jnp.concatenate(parts, axis=1)
    act = _silu(hid[:, :D_MODEL]) * hid[:, D_MODEL:]
    _store_rows(o_ref, jnp.dot(act.astype(BF16), wdb[...], preferred_element_type=F32))

    @pl.when(i == last)
    def _():
        for t in range(bm):
            gather(tokn_ref, 1 - slot, t).wait()


def _combine_kernel(dest_ref, destn_ref, xn_ref, wc_ref, mod_ref, ln_ref, ys_ref, o_ref, buf, sem):
    i = pl.program_id(0)
    last = pl.num_programs(0) - 1
    tm = xn_ref.shape[0]
    slot = lax.rem(i, 2)

    def gather(dref, slot_, k, t):
        return pltpu.make_async_copy(ys_ref.at[pl.ds(dref[0, 0, k * tm + t], 1)], buf.at[slot_, k, pl.ds(t, 1)],
                                     sem.at[slot_])

    def for_all(fn):
        for k in range(TOP_K):
            for t in range(tm):
                fn(k, t)

    @pl.when(i == 0)
    def _():
        for_all(lambda k, t: gather(dest_ref, 0, k, t).start(priority=t % 2))

    for_all(lambda k, t: gather(destn_ref, 1 - slot, k, t).start(priority=t % 2))
    for_all(lambda k, t: gather(dest_ref, slot, k, t).wait())
    m = mod_ref[0]
    ln = ln_ref[...]
    wc = wc_ref[...]
    y = wc[:, 0:1] * _load_rows(buf.at[slot, 0]) + wc[:, 1:2] * _load_rows(buf.at[slot, 1])
    o_ref[...] = _layer_norm(DEEPNORM_ALPHA * xn_ref[...] + (1.0 + m[5:6]) * y, ln[0:1], ln[1:2])

    @pl.when(i == last)
    def _():
        for_all(lambda k, t: gather(destn_ref, 1 - slot, k, t).wait())


def _moe(xn, h2, rout, cnt, mod_l, ln_g, ln_b, w_up, w_down, layer, seq):
    ntok = xn.shape[0]
    d = D_MODEL
    tiles = ntok // ROW_TILE
    tps = seq // ROW_TILE
    bm = MOE_ROWS
    n_assign = ntok * TOP_K
    n_blocks = n_assign // bm + N_EXPERTS
    n_pad = n_blocks * bm

    e0 = rout[0].astype(jnp.int32)
    e1 = rout[1].astype(jnp.int32)
    counts = cnt[:, 0].astype(jnp.int32)
    padded = (counts + bm - 1) // bm * bm
    pad_end = jnp.cumsum(padded)
    pad_start = pad_end - padded
    dest0 = pad_start[e0] + rout[4].astype(jnp.int32)
    dest1 = pad_start[e1] + rout[5].astype(jnp.int32)
    dest = jnp.concatenate([dest0.reshape(tiles, 1, ROW_TILE), dest1.reshape(tiles, 1, ROW_TILE)], axis=-1)
    block_start = jnp.arange(n_blocks, dtype=jnp.int32) * bm
    block_expert = jnp.minimum(jnp.sum((block_start[:, None] >= pad_end[None, :]).astype(jnp.int32), axis=1),
                               N_EXPERTS - 1)
    order = jnp.argsort(jnp.stack([e0, e1], axis=1).reshape(-1)).astype(jnp.int32)
    first = block_start - pad_start[block_expert]
    src = (jnp.cumsum(counts) - counts)[block_expert] + first
    lane = jnp.arange(bm, dtype=jnp.int32)[None, :]
    valid = lane < (counts[block_expert] - first)[:, None]
    picked = order[jnp.clip(src[:, None] + lane, 0, n_assign - 1)] // TOP_K
    row_tok = jnp.where(valid, picked, 0).reshape(n_blocks, 1, bm)
    wcol = jnp.transpose(rout[2:4])

    tok_spec = lambda nxt: pl.BlockSpec((1, 1, bm), lambda i, be: (jnp.minimum(i + nxt, n_blocks - 1), 0, 0),
                                        memory_space=pltpu.SMEM)
    ys = pl.pallas_call(
        _ffn_kernel,
        grid_spec=pltpu.PrefetchScalarGridSpec(
            num_scalar_prefetch=1,
            grid=(n_blocks,),
            in_specs=[tok_spec(0), tok_spec(1), pl.BlockSpec(memory_space=pl.ANY),
                      pl.BlockSpec((1, 1, d, 2 * d), lambda i, be: (layer, be[i], 0, 0)),
                      pl.BlockSpec((1, 1, d, d), lambda i, be: (layer, be[i], 0, 0))],
            out_specs=_tile_rows_spec(bm),
            scratch_shapes=[pltpu.VMEM((2, bm, d // LANES, LANES), F32), pltpu.VMEM((d, 2 * d), BF16),
                            pltpu.VMEM((d, d), BF16), pltpu.SemaphoreType.DMA((2,))],
        ),
        out_shape=jax.ShapeDtypeStruct((n_pad, d // LANES, LANES), F32),
        compiler_params=_cparams("arbitrary"),
        name="moe_ffn",
    )(block_expert, row_tok, row_tok, h2, w_up, w_down)

    ln = _pad_to(jnp.stack([ln_g, ln_b]).astype(F32), 0, 8)
    dest_spec = lambda nxt: pl.BlockSpec((1, 1, 2 * ROW_TILE), lambda i: (jnp.minimum(i + nxt, tiles - 1), 0, 0),
                                         memory_space=pltpu.SMEM)
    return pl.pallas_call(
        _combine_kernel,
        grid=(tiles,),
        in_specs=[dest_spec(0), dest_spec(1), _row_spec(d), pl.BlockSpec((ROW_TILE, 2), lambda i: (i, 0)),
                  _mod_spec(tps), _full_spec((8, d)), pl.BlockSpec(memory_space=pl.ANY)],
        out_specs=_row_spec(d),
        out_shape=jax.ShapeDtypeStruct((ntok, d), F32),
        scratch_shapes=[pltpu.VMEM((2, TOP_K, ROW_TILE, d // LANES, LANES), F32), pltpu.SemaphoreType.DMA((2,))],
        compiler_params=_cparams("arbitrary"),
        name="moe_combine",
    )(dest, dest, xn, wcol, mod_l, ln, ys)


def kernel(x, c, ada_w, ada_b, ln1_g, ln1_b, ln2_g, ln2_b, ssd_w_in, ssd_conv_w, ssd_conv_b, ssd_dt_bias, ssd_a_log, ssd_d, ssd_norm_g, ssd_w_out, gla_w_in, gla_w_gate, gla_b_gate, gla_norm_g, gla_w_out, rwkv_mu, rwkv_w_in, rwkv_w0, rwkv_w_w1, rwkv_w_w2, rwkv_a0, rwkv_w_a1, rwkv_w_a2, rwkv_w_g1, rwkv_w_g2, rwkv_k_k, rwkv_k_a, rwkv_r_k, rwkv_gn_g, rwkv_gn_b, rwkv_w_out, s5_w_in, s5_a_re, s5_a_im, s5_log_dt, s5_b_re, s5_b_im, s5_c_re, s5_c_im, s5_d, s5_w_glu, s5_w_out, moe_w_up, moe_w_down, router_w, router_b):
    bsz, seq, d = x.shape
    assert d == D_MODEL and seq % ROW_TILE == 0 and seq % SSD_CHUNK == 0
    depth = ada_w.shape[0]
    mod = _adaln_mod(c, ada_w, ada_b).reshape(depth, bsz, 6, d)
    rw = _hilo(jnp.transpose(router_w))
    rb = router_b.astype(F32).reshape(N_EXPERTS, 1)
    x2 = x.reshape(bsz * seq, d)
    for i in range(depth):
        kind, j = i % 4, i // 4
        mod_l = mod[i]
        if kind == 0:
            y = _ssd_mixer(x2, mod_l, bsz, seq, ssd_w_in[j], ssd_conv_w[j], ssd_conv_b[j], ssd_dt_bias[j],
                           ssd_a_log[j], ssd_d[j], ssd_norm_g[j])
            w_out = ssd_w_out[j]
        elif kind == 1:
            y = _gla_mixer(x2, mod_l, bsz, seq, gla_w_in[j], gla_w_gate[j], gla_b_gate[j], gla_norm_g[j])
            w_out = gla_w_out[j]
        elif kind == 2:
            y = _rwkv_mixer(x2, mod_l, bsz, seq, rwkv_mu[j], rwkv_w_in[j], rwkv_w0[j], rwkv_w_w1[j], rwkv_w_w2[j],
                            rwkv_a0[j], rwkv_w_a1[j], rwkv_w_a2[j], rwkv_w_g1[j], rwkv_w_g2[j], rwkv_k_k[j],
                            rwkv_k_a[j], rwkv_r_k[j], rwkv_gn_g[j], rwkv_gn_b[j])
            w_out = rwkv_w_out[j]
        else:
            y = _s5_mixer(x2, mod_l, bsz, seq, s5_w_in[j], s5_a_re[j], s5_a_im[j], s5_log_dt[j], s5_b_re[j],
                          s5_b_im[j], s5_c_re[j], s5_c_im[j], s5_d[j], s5_w_glu[j])
            w_out = s5_w_out[j]
        xn, h2, rout, cnt = _post(x2, y, w_out, mod_l, ln1_g[i], ln1_b[i], rw, rb, seq)
        x2 = _moe(xn, h2, rout, cnt, mod_l, ln2_g[i], ln2_b[i], moe_w_up, moe_w_down, i, seq)
    return x2.reshape(bsz, seq, d)
```

```python
import functools
import math

import jax
import jax.numpy as jnp
from jax import lax
from jax.experimental import pallas as pl
from jax.experimental.pallas import tpu as pltpu

F32 = jnp.float32
BF16 = jnp.bfloat16

D_MODEL = 1024
DEPTH = 4
DEEPNORM_ALPHA = (2 * DEPTH) ** 0.25
LN_EPS = 1e-5
RMS_EPS = 1e-5

SSD_D_INNER = 2048
SSD_HEAD_DIM = 64
SSD_N_HEADS = 32
SSD_N_GROUPS = 4
SSD_D_STATE = 128
SSD_CONV = 4
SSD_CONV_DIM = SSD_D_INNER + 2 * SSD_N_GROUPS * SSD_D_STATE
SSD_CHUNK = 128

GLA_N_HEADS = 4
GLA_D_K = 512
GLA_D_V = 1024
GLA_GATE_RANK = 16
GLA_TAU = 16.0
GLA_SUB = 16
GLA_ROWS = 256

RWKV_HEAD_DIM = 64
RWKV_N_HEADS = 16
RWKV_GN_EPS = 64e-5
RWKV_CHUNK = 64
RWKV_GROUP = 4
RWKV_SEQS = 2

S5_GROUP = 16
S5_N_GROUPS = 64
S5_STATE = 64
S5_CHUNK = 16
S5_PACK = 8
S5_STEP_CHUNKS = 16

N_EXPERTS = 16
N_EXPERT_GROUPS = 4
EXPERTS_PER_GROUP = 4
TOP_K = 2
MOE_ROWS = 256

ROW_TILE = 256
POST_TILE = 1024
LANES = 128
VMEM_LIMIT = 56 * 1024 * 1024
NEG_BIG = -1e30


def _cparams(*sem):
    return pltpu.CompilerParams(dimension_semantics=sem, vmem_limit_bytes=VMEM_LIMIT)


def _dot(a, b):
    return jnp.dot(a.astype(BF16), b.astype(BF16), preferred_element_type=F32)


def _dot_nt(a, b):
    return lax.dot_general(a.astype(BF16), b.astype(BF16), (((1,), (1,)), ((), ())), preferred_element_type=F32)


def _dot_tn(a, b):
    return lax.dot_general(a.astype(BF16), b.astype(BF16), (((0,), (0,)), ((), ())), preferred_element_type=F32)


def _split(x):
    hi = x.astype(BF16)
    lo = (x - hi.astype(F32)).astype(BF16)
    return hi, lo


def _split3(x):
    h1 = x.astype(BF16)
    r1 = x - h1.astype(F32)
    h2 = r1.astype(BF16)
    h3 = (r1 - h2.astype(F32)).astype(BF16)
    return h1, h2, h3


def _dot_x2(a, b_exact):
    hi, lo = _split(a)
    return _dot(hi, b_exact) + _dot(lo, b_exact)


def _dot_x3(a, b_exact):
    h1, h2, h3 = _split3(a)
    return _dot(h1, b_exact) + _dot(h2, b_exact) + _dot(h3, b_exact)


def _dot_hl(a_hi, a_lo, b_hi, b_lo):
    return _dot(a_hi, b_hi) + _dot(a_lo, b_hi) + _dot(a_hi, b_lo)


def _sigmoid(x):
    return 1.0 / (1.0 + jnp.exp(-x))


def _silu(x):
    return x * _sigmoid(x)


def _softplus(x):
    return jnp.maximum(x, 0.0) + jnp.log1p(jnp.exp(-jnp.abs(x)))


def _hilo(w):
    hi, lo = _split(w.astype(F32))
    return jnp.stack([hi, lo])


def _pad_to(a, axis, size):
    pad = [(0, 0)] * a.ndim
    pad[axis] = (0, size - a.shape[axis])
    return jnp.pad(a, pad)


def _modulate(x, m, shift_row, scale_row):
    return x * (1.0 + m[scale_row:scale_row + 1]) + m[shift_row:shift_row + 1]


def _layer_norm(v, g, b):
    mu = jnp.mean(v, axis=-1, keepdims=True)
    vc = v - mu
    var = jnp.mean(vc * vc, axis=-1, keepdims=True)
    return vc * lax.rsqrt(var + LN_EPS) * g + b


def _mod_kernel(c_ref, w_ref, b_ref, o_ref):
    o_ref[0] = _dot(_silu(c_ref[...]), w_ref[0]) + b_ref[0]


def _adaln_mod(c, ada_w, ada_b):
    nl, d, n6 = ada_w.shape
    bsz = c.shape[0]
    tn = 512
    return pl.pallas_call(
        _mod_kernel,
        grid=(nl, n6 // tn),
        in_specs=[
            pl.BlockSpec((bsz, d), lambda l, j: (0, 0)),
            pl.BlockSpec((1, d, tn), lambda l, j: (l, 0, j)),
            pl.BlockSpec((1, 1, tn), lambda l, j: (l, 0, j)),
        ],
        out_specs=pl.BlockSpec((1, bsz, tn), lambda l, j: (l, 0, j)),
        out_shape=jax.ShapeDtypeStruct((nl, bsz, n6), F32),
        compiler_params=_cparams("parallel", "parallel"),
        name="adaln_mod",
    )(c, ada_w, ada_b.reshape(nl, 1, n6))


def _row_spec(width, tm=ROW_TILE):
    return pl.BlockSpec((tm, width), lambda i: (i, 0))


def _full_spec(shape):
    nd = len(shape)
    return pl.BlockSpec(shape, lambda *_: (0,) * nd)


def _mod_spec(tiles_per_seq):
    return pl.BlockSpec((1, 6, D_MODEL), lambda i: (i // tiles_per_seq, 0, 0))


def _ssd_in_kernel(x_ref, mod_ref, wz_ref, wx_ref, wdt_ref, z_ref, xbc_ref, dt_ref):
    h = _modulate(x_ref[...], mod_ref[0], 0, 1)
    hb, hl = _split(h)
    z_ref[...] = jnp.dot(hb, wz_ref[...], preferred_element_type=F32).astype(BF16)
    xbc_ref[...] = jnp.dot(hb, wx_ref[...], preferred_element_type=F32)
    dt_ref[...] = _dot_hl(hb, hl, wdt_ref[0], wdt_ref[1])


def _ssd_scan_kernel(z_ref, xbc_ref, dt_ref, cw_ref, cb_ref, dtb_ref, an_ref, dsk_ref, ng_ref,
                     e_ref, tri_ref, triu_ref, o_ref, xpad, state):
    c = pl.program_id(1)
    L = z_ref.shape[0]
    nstate = SSD_D_STATE

    @pl.when(c == 0)
    def _():
        state[...] = jnp.zeros(state.shape, F32)
        xpad[0:8, :] = jnp.zeros((8, SSD_CONV_DIM), F32)

    @pl.when(c != 0)
    def _():
        xpad[0:8, :] = xpad[L:L + 8, :]

    xpad[8:8 + L, :] = xbc_ref[...]

    acc = cb_ref[...] + cw_ref[3:4, :] * xpad[8:8 + L, :]
    for j in range(1, SSD_CONV):
        acc = acc + cw_ref[3 - j:4 - j, :] * xpad[pl.ds(8 - j, L), :]
    act = _silu(acc)
    xs = act[:, :SSD_D_INNER]
    bm = act[:, SSD_D_INNER:SSD_D_INNER + SSD_N_GROUPS * nstate]
    cm = act[:, SSD_D_INNER + SSD_N_GROUPS * nstate:]

    dtv = _softplus(dt_ref[...] + dtb_ref[...])
    dta = dtv * an_ref[...]
    dhi, dlo = _split(dta)
    tri = tri_ref[...]
    triu = triu_ref[...]
    acum = _dot(tri, dhi) + _dot(tri, dlo)
    acum_t = _dot_tn(dhi, triu) + _dot_tn(dlo, triu)

    e = e_ref[...]
    acum_x = _dot_x3(acum, e)
    dt_x = _dot_x3(dtv, e)
    eac_x = jnp.exp(acum_x)
    last_x = acum_x[L - 1:L, :]
    elast_x = jnp.exp(last_x)
    xdt = xs * dt_x
    xw = xs * (jnp.exp(last_x - acum_x) * dt_x)

    row = lax.broadcasted_iota(jnp.int32, (L, L), 0)
    col = lax.broadcasted_iota(jnp.int32, (L, L), 1)
    causal = row >= col
    lane = lax.broadcasted_iota(jnp.int32, (L, LANES), 1)
    left = lane < SSD_HEAD_DIM

    pieces = []
    heads_per_group = SSD_N_HEADS // SSD_N_GROUPS
    for g in range(SSD_N_GROUPS):
        bg = bm[:, g * nstate:(g + 1) * nstate]
        cg = cm[:, g * nstate:(g + 1) * nstate]
        cb = _dot_nt(cg, bg)
        for p in range(heads_per_group // 2):
            h0 = g * heads_per_group + 2 * p
            pair = h0 // 2
            sl = slice(h0 * SSD_HEAD_DIM, h0 * SSD_HEAD_DIM + LANES)
            ms = []
            for hh in (h0, h0 + 1):
                seg = acum[:, hh:hh + 1] - acum_t[hh:hh + 1, :]
                ms.append((cb * jnp.exp(jnp.where(causal, seg, NEG_BIG))).astype(BF16))
            mcat = jnp.concatenate(ms, axis=1)
            xp = xdt[:, sl]
            xbd = jnp.concatenate([jnp.where(left, xp, 0.0), jnp.where(left, 0.0, xp)], axis=0)
            st = state[pair]
            y_pair = _dot(mcat, xbd) + _dot(cg, st) * eac_x[:, sl]
            state[pair] = st * elast_x[:, sl] + _dot_tn(bg, xw[:, sl])
            pieces.append(y_pair)
    y = jnp.concatenate(pieces, axis=1) + dsk_ref[...] * xs
    y = y * _silu(z_ref[...].astype(F32))
    gw = SSD_D_INNER // SSD_N_GROUPS
    outs = []
    for g in range(SSD_N_GROUPS):
        yg = y[:, g * gw:(g + 1) * gw]
        outs.append(yg * lax.rsqrt(jnp.mean(yg * yg, axis=-1, keepdims=True) + RMS_EPS))
    o_ref[...] = (jnp.concatenate(outs, axis=1) * ng_ref[...]).astype(BF16)


def _ssd_mixer(x2, mod_l, bsz, seq, w_in, conv_w, conv_b, dt_bias, a_log, d_skip, norm_g):
    ntok = x2.shape[0]
    tiles = ntok // ROW_TILE
    tps = seq // ROW_TILE
    wz = w_in[:, :SSD_D_INNER].astype(BF16)
    wx = w_in[:, SSD_D_INNER:SSD_D_INNER + SSD_CONV_DIM].astype(BF16)
    wdt = _hilo(_pad_to(w_in[:, SSD_D_INNER + SSD_CONV_DIM:], 1, LANES))
    z, xbc, dt = pl.pallas_call(
        _ssd_in_kernel,
        grid=(tiles,),
        in_specs=[_row_spec(D_MODEL), _mod_spec(tps), _full_spec(wz.shape), _full_spec(wx.shape),
                  _full_spec(wdt.shape)],
        out_specs=[_row_spec(SSD_D_INNER), _row_spec(SSD_CONV_DIM), _row_spec(LANES)],
        out_shape=[jax.ShapeDtypeStruct((ntok, SSD_D_INNER), BF16),
                   jax.ShapeDtypeStruct((ntok, SSD_CONV_DIM), F32),
                   jax.ShapeDtypeStruct((ntok, LANES), F32)],
        compiler_params=_cparams("parallel"),
        name="ssd_in",
    )(x2, mod_l, wz, wx, wdt)

    L = SSD_CHUNK
    nc = seq // L
    a_neg = _pad_to((-jnp.exp(a_log.astype(F32))).reshape(1, SSD_N_HEADS), 1, LANES)
    dtb = _pad_to(dt_bias.astype(F32).reshape(1, SSD_N_HEADS), 1, LANES)
    dsk = jnp.repeat(d_skip.astype(F32), SSD_HEAD_DIM).reshape(1, SSD_D_INNER)
    expand = (jnp.arange(LANES)[:, None] == (jnp.arange(SSD_D_INNER) // SSD_HEAD_DIM)[None, :]).astype(BF16)
    tri = jnp.tril(jnp.ones((L, L), BF16))
    triu = jnp.triu(jnp.ones((L, L), BF16))
    chunk = lambda w: pl.BlockSpec((L, w), lambda b, c: (b * nc + c, 0))
    full2 = lambda shape: pl.BlockSpec(shape, lambda b, c: (0, 0))
    return pl.pallas_call(
        _ssd_scan_kernel,
        grid=(bsz, nc),
        in_specs=[chunk(SSD_D_INNER), chunk(SSD_CONV_DIM), chunk(LANES),
                  full2((SSD_CONV, SSD_CONV_DIM)), full2((1, SSD_CONV_DIM)), full2((1, LANES)), full2((1, LANES)),
                  full2((1, SSD_D_INNER)), full2((1, SSD_D_INNER)), full2((LANES, SSD_D_INNER)),
                  full2((L, L)), full2((L, L))],
        out_specs=chunk(SSD_D_INNER),
        out_shape=jax.ShapeDtypeStruct((ntok, SSD_D_INNER), BF16),
        scratch_shapes=[pltpu.VMEM((L + 8, SSD_CONV_DIM), F32),
                        pltpu.VMEM((SSD_N_HEADS // 2, SSD_D_STATE, LANES), F32)],
        compiler_params=_cparams("parallel", "arbitrary"),
        name="ssd_scan",
    )(z, xbc, dt, conv_w.astype(F32), conv_b.astype(F32).reshape(1, -1), dtb, a_neg, dsk,
      norm_g.astype(F32).reshape(1, -1), expand, tri, triu)


def _gla_in_kernel(x_ref, mod_ref, wq_ref, wk_ref, wv_ref, wr_ref, wg_ref, wgate_ref, bgate_ref,
                   q_ref, k_ref, v_ref, r_ref, la_ref):
    h = _modulate(x_ref[...], mod_ref[0], 0, 1)
    hb, hl = _split(h)
    dk = GLA_D_K // GLA_N_HEADS
    q_ref[...] = jnp.dot(hb, wq_ref[...], preferred_element_type=F32) * (dk ** -0.5)
    k_ref[...] = jnp.dot(hb, wk_ref[...], preferred_element_type=F32)
    v_ref[...] = jnp.dot(hb, wv_ref[...], preferred_element_type=F32)
    r_ref[...] = jnp.dot(hb, wr_ref[...], preferred_element_type=F32).astype(BF16)
    g_low = _dot_hl(hb, hl, wg_ref[0], wg_ref[1])
    gh, gl = _split(g_low)
    gz = _dot_hl(gh, gl, wgate_ref[0], wgate_ref[1]) + bgate_ref[...]
    la_ref[...] = (jnp.minimum(gz, 0.0) - jnp.log1p(jnp.exp(-jnp.abs(gz)))) * (1.0 / GLA_TAU)


def _gla_scan_kernel(q_ref, k_ref, v_ref, r_ref, la_ref, ng_ref, tri_ref, o_ref, state):
    c = pl.program_id(1)
    rows = q_ref.shape[0]
    sub = GLA_SUB
    nsub = rows // sub
    dk = GLA_D_K // GLA_N_HEADS
    dv = GLA_D_V // GLA_N_HEADS

    @pl.when(c == 0)
    def _():
        state[...] = jnp.zeros(state.shape, F32)

    q = q_ref[...]
    k = k_ref[...]
    v = v_ref[...]
    lhi, llo = _split(la_ref[...])
    tri = tri_ref[...]
    bcum = _dot(tri, lhi) + _dot(tri, llo)

    def sub_row(a, j):
        a3 = a.reshape(nsub, sub, a.shape[-1])
        return jnp.broadcast_to(a3[:, j:j + 1, :], a3.shape).reshape(a.shape)

    last = sub_row(bcum, sub - 1)
    qdec = q * jnp.exp(bcum)
    kdec = k * jnp.exp(last - bcum)
    elast = jnp.exp(last)

    lane = lax.broadcasted_iota(jnp.int32, (rows, LANES), 1)
    local_row = lax.broadcasted_iota(jnp.int32, (rows, LANES), 0) & (sub - 1)
    att = jnp.zeros((rows, LANES), F32)
    for h in range(GLA_N_HEADS):
        ks = slice(h * dk, (h + 1) * dk)
        qh, kh, bh = q[:, ks], k[:, ks], bcum[:, ks]
        for j in range(sub):
            ex = jnp.exp(jnp.minimum(bh - sub_row(bh, j), 0.0))
            a = jnp.sum(qh * sub_row(kh, j) * ex, axis=-1, keepdims=True)
            att = jnp.where(lane == h * sub + j, a, att)
    att = jnp.where((lane & (sub - 1)) <= local_row, att, 0.0)
    ahi, alo = _split(att)

    ng = ng_ref[...]
    value_head = lax.broadcasted_iota(jnp.int32, (sub, GLA_D_V), 1) >> (dv.bit_length() - 1)
    for s in range(nsub):
        rs = slice(s * sub, (s + 1) * sub)
        vs = v[rs]
        vbd = jnp.concatenate([jnp.where(value_head == h, vs, 0.0) for h in range(GLA_N_HEADS)], axis=0)
        used = GLA_N_HEADS * sub
        o_intra = _dot(ahi[rs, :used], vbd) + _dot(alo[rs, :used], vbd)
        outs = []
        for h in range(GLA_N_HEADS):
            ks = slice(h * dk, (h + 1) * dk)
            st = state[h]
            oh = o_intra[:, h * dv:(h + 1) * dv] + _dot_nt(qdec[rs, ks], st)
            state[h] = st * elast[s * sub:s * sub + 1, ks] + _dot_tn(vs[:, h * dv:(h + 1) * dv], kdec[rs, ks])
            outs.append(oh * lax.rsqrt(jnp.mean(oh * oh, axis=-1, keepdims=True) + RMS_EPS))
        on = jnp.concatenate(outs, axis=1) * ng
        o_ref[rs, :] = (on * _silu(r_ref[rs, :].astype(F32))).astype(BF16)


def _gla_mixer(x2, mod_l, bsz, seq, w_in, w_gate, b_gate, norm_g):
    ntok = x2.shape[0]
    tiles = ntok // ROW_TILE
    tps = seq // ROW_TILE
    o1, o2, o3 = GLA_D_K, 2 * GLA_D_K, 2 * GLA_D_K + GLA_D_V
    o4 = o3 + GLA_D_V
    wq, wk = w_in[:, :o1].astype(BF16), w_in[:, o1:o2].astype(BF16)
    wv, wr = w_in[:, o2:o3].astype(BF16), w_in[:, o3:o4].astype(BF16)
    wg = _hilo(_pad_to(w_in[:, o4:], 1, LANES))
    wgate = _hilo(_pad_to(w_gate, 0, LANES))
    q, k, v, r, la = pl.pallas_call(
        _gla_in_kernel,
        grid=(tiles,),
        in_specs=[_row_spec(D_MODEL), _mod_spec(tps), _full_spec(wq.shape), _full_spec(wk.shape),
                  _full_spec(wv.shape), _full_spec(wr.shape), _full_spec(wg.shape), _full_spec(wgate.shape),
                  _full_spec((1, GLA_D_K))],
        out_specs=[_row_spec(GLA_D_K), _row_spec(GLA_D_K), _row_spec(GLA_D_V), _row_spec(GLA_D_V),
                   _row_spec(GLA_D_K)],
        out_shape=[jax.ShapeDtypeStruct((ntok, GLA_D_K), F32), jax.ShapeDtypeStruct((ntok, GLA_D_K), F32),
                   jax.ShapeDtypeStruct((ntok, GLA_D_V), F32), jax.ShapeDtypeStruct((ntok, GLA_D_V), BF16),
                   jax.ShapeDtypeStruct((ntok, GLA_D_K), F32)],
        compiler_params=_cparams("parallel"),
        name="gla_in",
    )(x2, mod_l, wq, wk, wv, wr, wg, wgate, b_gate.astype(F32).reshape(1, -1))

    rows = GLA_ROWS
    nc = seq // rows
    chunk = lambda w: pl.BlockSpec((rows, w), lambda b, c: (b * nc + c, 0))
    full2 = lambda shape: pl.BlockSpec(shape, lambda b, c: (0, 0))
    idx = jnp.arange(rows)
    tri = ((idx[:, None] >= idx[None, :]) & (idx[:, None] // GLA_SUB == idx[None, :] // GLA_SUB)).astype(BF16)
    return pl.pallas_call(
        _gla_scan_kernel,
        grid=(bsz, nc),
        in_specs=[chunk(GLA_D_K), chunk(GLA_D_K), chunk(GLA_D_V), chunk(GLA_D_V), chunk(GLA_D_K),
                  full2((1, GLA_D_V)), full2((rows, rows))],
        out_specs=chunk(GLA_D_V),
        out_shape=jax.ShapeDtypeStruct((ntok, GLA_D_V), BF16),
        scratch_shapes=[pltpu.VMEM((GLA_N_HEADS, GLA_D_V // GLA_N_HEADS, GLA_D_K // GLA_N_HEADS), F32)],
        compiler_params=_cparams("parallel", "arbitrary"),
        name="gla_scan",
    )(q, k, v, r, la, norm_g.astype(F32).reshape(1, -1), tri)


def _head_sum(x, sel, sel_t):
    return _dot_x3(_dot_x2(x, sel), sel_t)


def _rwkv_in_kernel(x_ref, mod_ref, mu_ref, wr_ref, wk_ref, wv_ref, ww1_ref, ww2_ref, wa1_ref, wa2_ref,
                    wg1_ref, wg2_ref, vec_ref, sel_ref, selt_ref,
                    r_ref, lw_ref, k2_ref, v_ref, kk_ref, b_ref, g_ref, bonus_ref, hp, *, tiles_per_seq):
    i = pl.program_id(0)
    tm = x_ref.shape[0]
    h = _modulate(x_ref[...], mod_ref[0], 0, 1)

    @pl.when(i % tiles_per_seq == 0)
    def _():
        hp[0:8, :] = jnp.zeros((8, D_MODEL), F32)

    @pl.when(i % tiles_per_seq != 0)
    def _():
        hp[0:8, :] = hp[tm:tm + 8, :]

    hp[8:8 + tm, :] = h
    xx = hp[pl.ds(7, tm), :] - h
    mu = mu_ref[...]
    xr, xw, xk, xv, xa, xg = (h + xx * mu[j:j + 1] for j in range(6))
    vec = vec_ref[...]
    w0, a0, k_k, k_a, r_k = (vec[j:j + 1] for j in range(5))
    sel = sel_ref[...]
    sel_t = selt_ref[...]

    r = _dot(xr, wr_ref[...])
    k = _dot(xk, wk_ref[...])
    v = _dot(xv, wv_ref[...])
    wlog = -_softplus(-(w0 + _dot(jnp.tanh(_dot(xw, ww1_ref[...])), ww2_ref[...]))) - 0.5
    a = _sigmoid(a0 + _dot(_dot(xa, wa1_ref[...]), wa2_ref[...]))
    g = _dot(_sigmoid(_dot(xg, wg1_ref[...])), wg2_ref[...])
    kx = k * k_k
    norm = jnp.sqrt(_head_sum(kx * kx, sel, sel_t))
    kk = kx / jnp.maximum(norm, 1e-12)
    k2 = k * (1.0 + (a - 1.0) * k_a)
    bonus = _head_sum(r * k2 * r_k, sel, sel_t) * v

    r_ref[...] = r.astype(BF16)
    lw_ref[...] = -jnp.exp(wlog)
    k2_ref[...] = k2.astype(BF16)
    v_ref[...] = v.astype(BF16)
    kk_ref[...] = kk.astype(BF16)
    b_ref[...] = (kk * a).astype(BF16)
    g_ref[...] = g.astype(BF16)
    bonus_ref[...] = bonus.astype(BF16)


def _rwkv_scan_kernel(r_ref, lw_ref, k2_ref, v_ref, kk_ref, b_ref, g_ref, bonus_ref, gn_ref, tri_ref, sel_ref,
                      selt_ref, o_ref, state):
    c = pl.program_id(1)
    C = RWKV_CHUNK
    n = RWKV_HEAD_DIM
    gw = RWKV_GROUP * n
    ngroups = D_MODEL // gw

    @pl.when(c == 0)
    def _():
        state[...] = jnp.zeros(state.shape, F32)

    tri = tri_ref[...]
    ri = lax.broadcasted_iota(jnp.int32, (gw, gw), 0)
    ci = lax.broadcasted_iota(jnp.int32, (gw, gw), 1)
    shift = n.bit_length() - 1
    same = (ri >> shift) == (ci >> shift)
    strict = same & ((ri & (C - 1)) > (ci & (C - 1)))
    incl = same & ((ri & (C - 1)) >= (ci & (C - 1)))
    lane_head = lax.broadcasted_iota(jnp.int32, (C, gw), 1) >> shift

    def expand(a):
        return jnp.concatenate([jnp.where(lane_head == hh, a, 0.0) for hh in range(RWKV_GROUP)], axis=0)

    def collapse(a):
        out = a[0:C]
        for hh in range(1, RWKV_GROUP):
            out = out + a[hh * C:(hh + 1) * C]
        return out

    nseq = r_ref.shape[0]
    groups = range(nseq * ngroups)
    unit = [(gi // ngroups, gi % ngroups) for gi in groups]
    sls = [(s, slice(None), slice(g * gw, (g + 1) * gw)) for s, g in unit]
    st = [state[s, g] for s, g in unit]
    lw = [lw_ref[sl] for sl in sls]
    r = [r_ref[sl].astype(F32) for sl in sls]
    k2 = [k2_ref[sl].astype(F32) for sl in sls]
    v = [v_ref[sl] for sl in sls]
    kk = [kk_ref[sl].astype(F32) for sl in sls]
    b = [b_ref[sl].astype(F32) for sl in sls]
    G = []
    for gi in groups:
        lhi, llo = _split(lw[gi])
        G.append(_dot(tri, lhi) + _dot(tri, llo))
    gc = [g[C - 1:C, :] for g in G]
    e_ng = [jnp.exp(-g) for g in G]
    e_c = [jnp.exp(gc[gi] - G[gi]) for gi in groups]
    rt = [(r[gi] * jnp.exp(G[gi])).astype(BF16) for gi in groups]
    kt = [(kk[gi] * jnp.exp(G[gi] - lw[gi])).astype(BF16) for gi in groups]
    kt_e = [expand(a) for a in kt]
    rt_e = [expand(a) for a in rt]
    kh_e = [expand((k2[gi] * e_ng[gi]).astype(BF16)) for gi in groups]
    bh_e = [expand((b[gi] * e_ng[gi]).astype(BF16)) for gi in groups]
    v_e = [expand(a) for a in v]
    p = [jnp.where(strict, -_dot_nt(kt_e[gi], bh_e[gi]), 0.0).astype(BF16) for gi in groups]
    m_kk = [jnp.where(strict, _dot_nt(kt_e[gi], kh_e[gi]), 0.0) for gi in groups]
    m_rk = [jnp.where(incl, _dot_nt(rt_e[gi], kh_e[gi]), 0.0) for gi in groups]
    m_rb = [jnp.where(incl, _dot_nt(rt_e[gi], bh_e[gi]), 0.0) for gi in groups]
    u = [_dot(m_kk[gi], v_e[gi]) + expand(_dot_nt(kt[gi], st[gi])) for gi in groups]
    for it in range(6):
        u = [u[gi] + _dot(p[gi], u[gi]) for gi in groups]
        if it < 5:
            p = [_dot(p[gi], p[gi]).astype(BF16) for gi in groups]
    ys = [collapse(_dot(m_rk[gi], v_e[gi]) - _dot(m_rb[gi], u[gi])) + _dot_nt(rt[gi], st[gi]) for gi in groups]
    for gi in groups:
        un = collapse(u[gi])
        upd = _dot_tn(v[gi], k2[gi] * e_c[gi]) - _dot_tn(un, b[gi] * e_c[gi])
        state[unit[gi]] = st[gi] * jnp.exp(gc[gi]) + jnp.where(same, upd, 0.0)
    y = jnp.concatenate([jnp.concatenate(ys[s * ngroups:(s + 1) * ngroups], axis=1) for s in range(nseq)], axis=0)
    sel = sel_ref[...]
    sel_t = selt_ref[...]
    mu = _head_sum(y, sel, sel_t) * (1.0 / n)
    yc = y - mu
    var = _head_sum(yc * yc, sel, sel_t) * (1.0 / n)
    gn = gn_ref[...]
    ygn = yc * lax.rsqrt(var + RWKV_GN_EPS) * gn[0:1] + gn[1:2]
    bonus = bonus_ref[...].reshape(nseq * C, D_MODEL).astype(F32)
    gate = g_ref[...].reshape(nseq * C, D_MODEL).astype(F32)
    o_ref[...] = ((ygn + bonus) * gate).astype(BF16).reshape(nseq, C, D_MODEL)


def _rwkv_mixer(x2, mod_l, bsz, seq, mu, w_in, w0, w_w1, w_w2, a0, w_a1, w_a2, w_g1, w_g2, k_k, k_a, r_k,
                gn_g, gn_b):
    ntok = x2.shape[0]
    tiles = ntok // ROW_TILE
    tps = seq // ROW_TILE
    d = D_MODEL
    wr, wk, wv = (w_in[j].astype(BF16) for j in range(3))
    ww1 = _pad_to(w_w1, 1, LANES).astype(BF16)
    ww2 = _pad_to(w_w2, 0, LANES).astype(BF16)
    wa1 = _pad_to(w_a1, 1, LANES).astype(BF16)
    wa2 = _pad_to(w_a2, 0, LANES).astype(BF16)
    wg1 = _pad_to(w_g1, 1, 2 * LANES).astype(BF16)
    wg2 = _pad_to(w_g2, 0, 2 * LANES).astype(BF16)
    vec = _pad_to(jnp.stack([w0, a0, k_k, k_a, r_k.reshape(d)]).astype(F32), 0, 8)
    sel = ((jnp.arange(d) // RWKV_HEAD_DIM)[:, None] == jnp.arange(LANES)[None, :]).astype(BF16)
    sel_t = jnp.transpose(sel)
    outs = pl.pallas_call(
        functools.partial(_rwkv_in_kernel, tiles_per_seq=tps),
        grid=(tiles,),
        in_specs=[_row_spec(d), _mod_spec(tps), _full_spec((6, d)), _full_spec((d, d)), _full_spec((d, d)),
                  _full_spec((d, d)), _full_spec(ww1.shape), _full_spec(ww2.shape), _full_spec(wa1.shape),
                  _full_spec(wa2.shape), _full_spec(wg1.shape), _full_spec(wg2.shape), _full_spec((8, d)),
                  _full_spec((d, LANES)), _full_spec((LANES, d))],
        out_specs=[_row_spec(d)] * 8,
        out_shape=[jax.ShapeDtypeStruct((ntok, d), F32 if j == 1 else BF16) for j in range(8)],
        scratch_shapes=[pltpu.VMEM((ROW_TILE + 8, d), F32)],
        compiler_params=_cparams("arbitrary"),
        name="rwkv_in",
    )(x2, mod_l, mu.astype(F32), wr, wk, wv, ww1, ww2, wa1, wa2, wg1, wg2, vec, sel, sel_t)
    r, lw, k2, v, kk, b, g, bonus = outs

    C = RWKV_CHUNK
    nc = seq // C
    nseq = RWKV_SEQS if bsz % RWKV_SEQS == 0 else 1
    chunk = pl.BlockSpec((nseq, C, d), lambda bb, c: (bb, c, 0))
    full2 = lambda shape: pl.BlockSpec(shape, lambda bb, c: (0, 0))
    gn = _pad_to(jnp.stack([gn_g, gn_b]).astype(F32), 0, 8)
    tri = jnp.tril(jnp.ones((C, C), BF16))
    gw = RWKV_GROUP * RWKV_HEAD_DIM
    by_seq = lambda a: a.reshape(bsz, seq, d)
    return pl.pallas_call(
        _rwkv_scan_kernel,
        grid=(bsz // nseq, nc),
        in_specs=[chunk] * 8 + [full2((8, d)), full2((C, C)), full2((d, LANES)), full2((LANES, d))],
        out_specs=chunk,
        out_shape=jax.ShapeDtypeStruct((bsz, seq, d), BF16),
        scratch_shapes=[pltpu.VMEM((nseq, d // gw, gw, gw), F32)],
        compiler_params=_cparams("parallel", "arbitrary"),
        name="rwkv_scan",
    )(*(by_seq(a) for a in (r, lw, k2, v, kk, b, g, bonus)), gn, tri, sel, sel_t).reshape(ntok, d)


def _s5_in_kernel(x_ref, mod_ref, w_ref, u_ref):
    h = _modulate(x_ref[...], mod_ref[0], 0, 1)
    u_ref[...] = _dot(h, w_ref[...])


def _s5_scan_kernel(u_ref, krow_ref, wst_ref, wout_ref, lam_ref, y_ref, kbig, lhs_sc, e_sc, xp_sc, xstate):
    q = pl.program_id(1)
    bsz, seq_rows, _ = u_ref.shape
    L = S5_CHUNK
    nq = seq_rows // L
    rows = bsz * nq

    @pl.when(q == 0)
    def _():
        xstate[...] = jnp.zeros(xstate.shape, F32)
        zero = jnp.zeros((LANES, LANES), BF16)
        for t in range(L):
            for s in range(L):
                blk = krow_ref[0, :, (t - s) * LANES:(t - s + 1) * LANES] if t >= s else zero
                kbig[t // 2, s * LANES:(s + 1) * LANES, (t % 2) * LANES:(t % 2 + 1) * LANES] = blk

    for s in range(L):
        lhs_sc[:, s * LANES:(s + 1) * LANES] = u_ref[:, pl.ds(s, nq, stride=L), :].reshape(rows, LANES).astype(BF16)
    e = jnp.dot(lhs_sc[...], wst_ref[0], preferred_element_type=F32)
    nslab = e_sc.shape[0]
    for j in range(nslab):
        e_sc[j] = e[:, j * LANES:(j + 1) * LANES]
    lam = lam_ref[0]
    l0 = lam[0:1]
    l1 = lam[1:2]
    half = xstate.shape[1] // 2

    def step(n, xst):
        sel = pl.ds(n, bsz, stride=nq)
        for j in range(nslab):
            xp_sc[j, sel, :] = xst[:, j * LANES:(j + 1) * LANES]
        e_n = jnp.concatenate([e_sc[j, sel, :] for j in range(nslab)], axis=1)
        return xst * l0 + pltpu.roll(xst, half, axis=1) * l1 + e_n

    xstate[...] = lax.fori_loop(0, nq, step, xstate[...])
    xprev = jnp.concatenate([xp_sc[j] for j in range(nslab)], axis=1).astype(BF16)

    def out_pair(i, carry):
        y2 = (jnp.dot(lhs_sc[...], kbig[i], preferred_element_type=F32)
              + jnp.dot(xprev, wout_ref[0, i], preferred_element_type=F32))
        for k in range(2):
            y_ref[:, pl.ds(2 * i + k, nq, stride=L), :] = y2[:, k * LANES:(k + 1) * LANES].reshape(bsz, nq, LANES)
        return carry

    lax.fori_loop(0, L // 2, out_pair, 0)


def _s5_glu_kernel(y_ref, u_ref, d_ref, w_ref, o_ref):
    y = y_ref[...] + d_ref[...] * u_ref[...]
    ge = 0.5 * y * (1.0 + jnp.tanh(math.sqrt(2.0 / math.pi) * (y + 0.044715 * (y * y * y))))
    ab = _dot(ge, w_ref[...])
    o_ref[...] = (ab[:, :D_MODEL] * _sigmoid(ab[:, D_MODEL:])).astype(BF16)


def _s5_operators(a_re, a_im, log_dt, b_re, b_im, c_re, c_im):
    hp = lax.Precision.HIGHEST
    L = S5_CHUNK
    a_re, a_im = a_re.astype(F32), a_im.astype(F32)
    dt = jnp.exp(log_dt.astype(F32))[:, None]
    mag = jnp.exp(a_re * dt)
    ab_re, ab_im = mag * jnp.cos(a_im * dt), mag * jnp.sin(a_im * dt)
    den = jnp.square(a_re) + jnp.square(a_im)
    f_re = ((ab_re - 1.0) * a_re + ab_im * a_im) / den
    f_im = (ab_im * a_re - (ab_re - 1.0) * a_im) / den
    b_re, b_im = b_re.astype(F32), b_im.astype(F32)
    bb_re = f_re[..., None] * b_re - f_im[..., None] * b_im
    bb_im = f_re[..., None] * b_im + f_im[..., None] * b_re
    tau = jnp.arange(L + 1, dtype=F32)[:, None, None]
    pmag = jnp.exp(tau * (a_re * dt)[None])
    pw_re = pmag * jnp.cos(tau * (a_im * dt)[None])
    pw_im = pmag * jnp.sin(tau * (a_im * dt)[None])
    c_re, c_im = c_re.astype(F32), c_im.astype(F32)
    cl_re = c_re[None] * pw_re[:, :, None, :] - c_im[None] * pw_im[:, :, None, :]
    cl_im = c_re[None] * pw_im[:, :, None, :] + c_im[None] * pw_re[:, :, None, :]
    kern = (jnp.einsum('tgcp,gpd->tgcd', cl_re, bb_re, precision=hp)
            - jnp.einsum('tgcp,gpd->tgcd', cl_im, bb_im, precision=hp))
    pk = S5_PACK
    npk = S5_N_GROUPS // pk
    eye = jnp.eye(pk, dtype=F32)
    krow = jnp.einsum('tGacd,ab->Gadtbc', kern[:L].reshape(L, npk, pk, S5_GROUP, S5_GROUP), eye)
    krow = krow.reshape(npk, pk * S5_GROUP, L * pk * S5_GROUP)
    rev_re = pw_re[L - 1 - jnp.arange(L)]
    rev_im = pw_im[L - 1 - jnp.arange(L)]
    st_re = rev_re[:, :, :, None] * bb_re[None] - rev_im[:, :, :, None] * bb_im[None]
    st_im = rev_re[:, :, :, None] * bb_im[None] + rev_im[:, :, :, None] * bb_re[None]
    st = jnp.stack([st_re, st_im]).reshape(2, L, npk, pk, S5_STATE, S5_GROUP)
    wst = jnp.einsum('qsGapd,ab->Gsadqbp', st, eye).reshape(npk, L * pk * S5_GROUP, 2 * pk * S5_STATE)
    cl = jnp.stack([cl_re[1:], -cl_im[1:]]).reshape(2, L, npk, pk, S5_GROUP, S5_STATE)
    wout = jnp.einsum('qtGacp,ab->Gqaptbc', cl, eye).reshape(npk, 2 * pk * S5_STATE, L // 2, 2 * pk * S5_GROUP)
    wout = jnp.transpose(wout, (0, 2, 1, 3))
    lr = pw_re[L].reshape(npk, pk * S5_STATE)
    li = pw_im[L].reshape(npk, pk * S5_STATE)
    lam = jnp.stack([jnp.concatenate([lr, lr], -1), jnp.concatenate([-li, li], -1)], axis=1)
    return krow.astype(BF16), wst.astype(BF16), wout.astype(BF16), lam


def _s5_mixer(x2, mod_l, bsz, seq, w_in, a_re, a_im, log_dt, b_re, b_im, c_re, c_im, d_skip, w_glu):
    ntok = x2.shape[0]
    tiles = ntok // ROW_TILE
    tps = seq // ROW_TILE
    d = D_MODEL
    u = pl.pallas_call(
        _s5_in_kernel,
        grid=(tiles,),
        in_specs=[_row_spec(d), _mod_spec(tps), _full_spec((d, d))],
        out_specs=_row_spec(d),
        out_shape=jax.ShapeDtypeStruct((ntok, d), F32),
        compiler_params=_cparams("parallel"),
        name="s5_in",
    )(x2, mod_l, w_in.astype(BF16))

    L = S5_CHUNK
    npk = S5_N_GROUPS // S5_PACK
    seq_rows = min(S5_STEP_CHUNKS * L, seq)
    nq = seq_rows // L
    width = L * LANES
    nstate = 2 * S5_PACK * S5_STATE
    krow, wst, wout, lam = _s5_operators(a_re, a_im, log_dt, b_re, b_im, c_re, c_im)
    blk = pl.BlockSpec((bsz, seq_rows, LANES), lambda g, q: (0, q, g))
    pack = lambda *shape: pl.BlockSpec((1,) + shape, lambda g, q: (g,) + (0,) * len(shape))
    y = pl.pallas_call(
        _s5_scan_kernel,
        grid=(npk, seq // seq_rows),
        in_specs=[blk, pack(LANES, width), pack(width, nstate), pack(L // 2, nstate, 2 * LANES), pack(2, nstate)],
        out_specs=blk,
        out_shape=jax.ShapeDtypeStruct((bsz, seq, d), F32),
        scratch_shapes=[pltpu.VMEM((L // 2, width, 2 * LANES), BF16), pltpu.VMEM((bsz * nq, width), BF16),
                        pltpu.VMEM((nstate // LANES, bsz * nq, LANES), F32),
                        pltpu.VMEM((nstate // LANES, bsz * nq, LANES), F32), pltpu.VMEM((bsz, nstate), F32)],
        compiler_params=_cparams("parallel", "arbitrary"),
        name="s5_scan",
    )(u.reshape(bsz, seq, d), krow, wst, wout, lam).reshape(ntok, d)

    return pl.pallas_call(
        _s5_glu_kernel,
        grid=(tiles,),
        in_specs=[_row_spec(d), _row_spec(d), _full_spec((1, d)), _full_spec((d, 2 * d))],
        out_specs=_row_spec(d),
        out_shape=jax.ShapeDtypeStruct((ntok, d), BF16),
        compiler_params=_cparams("parallel"),
        name="s5_glu",
    )(y, u, d_skip.astype(F32).reshape(1, d), w_glu.astype(BF16))


def _post_kernel(x_ref, y_ref, w_ref, mod_ref, ln_ref, rw_ref, rb_ref, tris_ref,
                 xn_ref, h2_ref, rout_ref, cnt_ref, run):
    i = pl.program_id(0)
    tm = x_ref.shape[0]

    @pl.when(i == 0)
    def _():
        run[...] = jnp.zeros(run.shape, F32)

    m = mod_ref[0]
    ln = ln_ref[...]
    y = jnp.dot(y_ref[...], w_ref[...], preferred_element_type=F32)
    xn = _layer_norm(DEEPNORM_ALPHA * x_ref[...] + (1.0 + m[2:3]) * y, ln[0:1], ln[1:2])
    xn_ref[...] = xn
    h2 = _modulate(xn, m, 3, 4)
    h2_ref[...] = h2

    hh, hl = _split(h2)
    logits = _dot_nt(rw_ref[0], hh) + _dot_nt(rw_ref[0], hl) + _dot_nt(rw_ref[1], hh)
    ex = jnp.exp(logits - jnp.max(logits, axis=0, keepdims=True))
    probs = ex / jnp.sum(ex, axis=0, keepdims=True)
    sel = probs + rb_ref[...]

    def first_max(vals):
        best = vals[0]
        for v in vals[1:]:
            best = jnp.maximum(best, v)
        idx = jnp.full(best.shape, len(vals) - 1, jnp.int32)
        for j in range(len(vals) - 2, -1, -1):
            idx = jnp.where(vals[j] == best, j, idx)
        return best, idx

    scores, firsts, seconds = [], [], []
    for g in range(N_EXPERT_GROUPS):
        s = [sel[g * EXPERTS_PER_GROUP + j:g * EXPERTS_PER_GROUP + j + 1, :] for j in range(EXPERTS_PER_GROUP)]
        v1, i1 = first_max(s)
        v2, i2 = first_max([jnp.where(i1 == j, NEG_BIG, s[j]) for j in range(EXPERTS_PER_GROUP)])
        scores.append(v1 + v2)
        firsts.append(i1)
        seconds.append(i2)
    _, grp = first_max(scores)
    e0 = jnp.zeros_like(grp)
    e1 = jnp.zeros_like(grp)
    for g in range(N_EXPERT_GROUPS):
        e0 = jnp.where(grp == g, g * EXPERTS_PER_GROUP + firsts[g], e0)
        e1 = jnp.where(grp == g, g * EXPERTS_PER_GROUP + seconds[g], e1)

    eidx = lax.broadcasted_iota(jnp.int32, (N_EXPERTS, tm), 0)
    hit0 = eidx == e0
    hit1 = eidx == e1
    w0 = jnp.sum(jnp.where(hit0, probs, 0.0), axis=0, keepdims=True)
    w1 = jnp.sum(jnp.where(hit1, probs, 0.0), axis=0, keepdims=True)
    wsum = w0 + w1
    onehot = hit0.astype(F32) + hit1.astype(F32)
    before = _dot(onehot, tris_ref[...]) + run[...]
    r0 = jnp.sum(jnp.where(hit0, before, 0.0), axis=0, keepdims=True)
    r1 = jnp.sum(jnp.where(hit1, before, 0.0), axis=0, keepdims=True)
    run[...] = run[...] + jnp.sum(onehot, axis=1, keepdims=True)
    cnt_ref[...] = jnp.broadcast_to(run[...], cnt_ref.shape)

    rid = lax.broadcasted_iota(jnp.int32, (8, tm), 0)
    rows = (e0.astype(F32), e1.astype(F32), w0 / wsum, w1 / wsum, r0, r1)
    out = jnp.zeros((8, tm), F32)
    for j, val in enumerate(rows):
        out = jnp.where(rid == j, val, out)
    rout_ref[...] = out


def _post(x2, y_in, w_out, mod_l, ln_g, ln_b, rw, rb, seq):
    ntok = x2.shape[0]
    tm = min(POST_TILE, seq)
    tiles = ntok // tm
    tps = seq // tm
    d = D_MODEL
    kin = y_in.shape[1]
    ln = _pad_to(jnp.stack([ln_g, ln_b]).astype(F32), 0, 8)
    tris = jnp.triu(jnp.ones((tm, tm), BF16), 1)
    return pl.pallas_call(
        _post_kernel,
        grid=(tiles,),
        in_specs=[_row_spec(d, tm), _row_spec(kin, tm), _full_spec((kin, d)), _mod_spec(tps), _full_spec((8, d)),
                  _full_spec((2, N_EXPERTS, d)), _full_spec((N_EXPERTS, 1)), _full_spec((tm, tm))],
        out_specs=[_row_spec(d, tm), _row_spec(d, tm), pl.BlockSpec((8, tm), lambda i: (0, i)),
                   _full_spec((N_EXPERTS, LANES))],
        out_shape=[jax.ShapeDtypeStruct((ntok, d), F32), jax.ShapeDtypeStruct((ntok, d), F32),
                   jax.ShapeDtypeStruct((8, ntok), F32), jax.ShapeDtypeStruct((N_EXPERTS, LANES), F32)],
        scratch_shapes=[pltpu.VMEM((N_EXPERTS, 1), F32)],
        compiler_params=_cparams("arbitrary"),
        name="post",
    )(x2, y_in, w_out.astype(BF16), mod_l, ln, rw, rb, tris)


def _ffn_kernel(be_ref, tok_ref, tokn_ref, h_ref, wu_ref, wd_ref, o_ref, xbuf, wub, wdb, sem):
    i = pl.program_id(0)
    last = pl.num_programs(0) - 1
    bm = o_ref.shape[0]
    slot = lax.rem(i, 2)
    rows = 128

    def gather(tok, slot_, t):
        return pltpu.make_async_copy(h_ref.at[pl.ds(tok[0, 0, t], 1)], xbuf.at[slot_, pl.ds(t, 1)], sem.at[slot_])

    @pl.when(i == 0)
    def _():
        for t in range(bm):
            gather(tok_ref, 0, t).start(priority=t % 2)

    prev = be_ref[jnp.maximum(i - 1, 0)]

    @pl.when((i == 0) | (be_ref[i] != prev))
    def _():
        def cast(j, carry):
            sl = pl.ds(pl.multiple_of(j * rows, rows), rows)
            wub[sl, :] = wu_ref[0, 0, sl, :].astype(BF16)
            wdb[sl, :] = wd_ref[0, 0, sl, :].astype(BF16)
            return carry
        lax.fori_loop(0, D_MODEL // rows, cast, 0)

    for t in range(bm):
        gather(tok_ref, slot, t).wait()
    x = xbuf[slot].astype(BF16)
    cols = 2 * LANES
    up_slabs = 2 * D_MODEL // cols
    down_slabs = D_MODEL // cols
    per = bm // (up_slabs + down_slabs)
    issued = [0]

    def issue(count):
        for t in range(issued[0], issued[0] + count):
            gather(tokn_ref, 1 - slot, t).start(priority=t % 2)
        issued[0] += count

    parts = []
    for c in range(up_slabs):
        parts.append(jnp.dot(x, wub[:, c * cols:(c + 1) * cols], preferred_element_type=F32))
        issue(per)
    hid = jnp.concatenate(parts, axis=1)
    act = (_silu(hid[:, :D_MODEL]) * hid[:, D_MODEL:]).astype(BF16)
    for c in range(down_slabs):
        o_ref[:, c * cols:(c + 1) * cols] = jnp.dot(act, wdb[:, c * cols:(c + 1) * cols],
                                                    preferred_element_type=F32)
        issue(per if c < down_slabs - 1 else bm - issued[0])

    @pl.when(i == last)
    def _():
        for t in range(bm):
            gather(tokn_ref, 1 - slot, t).wait()


def _combine_kernel(dest_ref, destn_ref, xn_ref, wc_ref, mod_ref, ln_ref, ys_ref, o_ref, buf, sem):
    i = pl.program_id(0)
    last = pl.num_programs(0) - 1
    tm = xn_ref.shape[0]
    slot = lax.rem(i, 2)

    def gather(dref, slot_, k, t):
        return pltpu.make_async_copy(ys_ref.at[pl.ds(dref[0, 0, k * tm + t], 1)], buf.at[slot_, k, pl.ds(t, 1)],
                                     sem.at[slot_])

    def for_all(fn):
        for k in range(TOP_K):
            for t in range(tm):
                fn(k, t)

    @pl.when(i == 0)
    def _():
        for_all(lambda k, t: gather(dest_ref, 0, k, t).start(priority=t % 2))

    for_all(lambda k, t: gather(destn_ref, 1 - slot, k, t).start(priority=t % 2))
    for_all(lambda k, t: gather(dest_ref, slot, k, t).wait())
    m = mod_ref[0]
    ln = ln_ref[...]
    wc = wc_ref[...]
    y = wc[:, 0:1] * buf[slot, 0] + wc[:, 1:2] * buf[slot, 1]
    o_ref[...] = _layer_norm(DEEPNORM_ALPHA * xn_ref[...] + (1.0 + m[5:6]) * y, ln[0:1], ln[1:2])

    @pl.when(i == last)
    def _():
        for_all(lambda k, t: gather(destn_ref, 1 - slot, k, t).wait())


def _moe(xn, h2, rout, cnt, mod_l, ln_g, ln_b, w_up, w_down, layer, seq):
    ntok = xn.shape[0]
    d = D_MODEL
    tiles = ntok // ROW_TILE
    tps = seq // ROW_TILE
    bm = MOE_ROWS
    n_assign = ntok * TOP_K
    n_blocks = n_assign // bm + N_EXPERTS
    n_pad = n_blocks * bm

    e0 = rout[0].astype(jnp.int32)
    e1 = rout[1].astype(jnp.int32)
    counts = cnt[:, 0].astype(jnp.int32)
    padded = (counts + bm - 1) // bm * bm
    pad_end = jnp.cumsum(padded)
    pad_start = pad_end - padded
    dest0 = pad_start[e0] + rout[4].astype(jnp.int32)
    dest1 = pad_start[e1] + rout[5].astype(jnp.int32)
    dest = jnp.concatenate([dest0.reshape(tiles, 1, ROW_TILE), dest1.reshape(tiles, 1, ROW_TILE)], axis=-1)
    block_start = jnp.arange(n_blocks, dtype=jnp.int32) * bm
    block_expert = jnp.minimum(jnp.sum((block_start[:, None] >= pad_end[None, :]).astype(jnp.int32), axis=1),
                               N_EXPERTS - 1)
    order = jnp.argsort(jnp.stack([e0, e1], axis=1).reshape(-1)).astype(jnp.int32)
    first = block_start - pad_start[block_expert]
    src = (jnp.cumsum(counts) - counts)[block_expert] + first
    lane = jnp.arange(bm, dtype=jnp.int32)[None, :]
    valid = lane < (counts[block_expert] - first)[:, None]
    picked = order[jnp.clip(src[:, None] + lane, 0, n_assign - 1)] // TOP_K
    row_tok = jnp.where(valid, picked, 0).reshape(n_blocks, 1, bm)
    wcol = jnp.transpose(rout[2:4])

    tok_spec = lambda nxt: pl.BlockSpec((1, 1, bm), lambda i, be: (jnp.minimum(i + nxt, n_blocks - 1), 0, 0),
                                        memory_space=pltpu.SMEM)
    ys = pl.pallas_call(
        _ffn_kernel,
        grid_spec=pltpu.PrefetchScalarGridSpec(
            num_scalar_prefetch=1,
            grid=(n_blocks,),
            in_specs=[tok_spec(0), tok_spec(1), pl.BlockSpec(memory_space=pl.ANY),
                      pl.BlockSpec((1, 1, d, 2 * d), lambda i, be: (layer, be[i], 0, 0)),
                      pl.BlockSpec((1, 1, d, d), lambda i, be: (layer, be[i], 0, 0))],
            out_specs=pl.BlockSpec((bm, d), lambda i, be: (i, 0)),
            scratch_shapes=[pltpu.VMEM((2, bm, d), F32), pltpu.VMEM((d, 2 * d), BF16),
                            pltpu.VMEM((d, d), BF16), pltpu.SemaphoreType.DMA((2,))],
        ),
        out_shape=jax.ShapeDtypeStruct((n_pad, d), F32),
        compiler_params=_cparams("arbitrary"),
        name="moe_ffn",
    )(block_expert, row_tok, row_tok, h2, w_up, w_down)

    ln = _pad_to(jnp.stack([ln_g, ln_b]).astype(F32), 0, 8)
    dest_spec = lambda nxt: pl.BlockSpec((1, 1, 2 * ROW_TILE), lambda i: (jnp.minimum(i + nxt, tiles - 1), 0, 0),
                                         memory_space=pltpu.SMEM)
    return pl.pallas_call(
        _combine_kernel,
        grid=(tiles,),
        in_specs=[dest_spec(0), dest_spec(1), _row_spec(d), pl.BlockSpec((ROW_TILE, 2), lambda i: (i, 0)),
                  _mod_spec(tps), _full_spec((8, d)), pl.BlockSpec(memory_space=pl.ANY)],
        out_specs=_row_spec(d),
        out_shape=jax.ShapeDtypeStruct((ntok, d), F32),
        scratch_shapes=[pltpu.VMEM((2, TOP_K, ROW_TILE, d), F32), pltpu.SemaphoreType.DMA((2,))],
        compiler_params=_cparams("arbitrary"),
        name="moe_combine",
    )(dest, dest, xn, wcol, mod_l, ln, ys)


def kernel(x, c, ada_w, ada_b, ln1_g, ln1_b, ln2_g, ln2_b, ssd_w_in, ssd_conv_w, ssd_conv_b, ssd_dt_bias, ssd_a_log, ssd_d, ssd_norm_g, ssd_w_out, gla_w_in, gla_w_gate, gla_b_gate, gla_norm_g, gla_w_out, rwkv_mu, rwkv_w_in, rwkv_w0, rwkv_w_w1, rwkv_w_w2, rwkv_a0, rwkv_w_a1, rwkv_w_a2, rwkv_w_g1, rwkv_w_g2, rwkv_k_k, rwkv_k_a, rwkv_r_k, rwkv_gn_g, rwkv_gn_b, rwkv_w_out, s5_w_in, s5_a_re, s5_a_im, s5_log_dt, s5_b_re, s5_b_im, s5_c_re, s5_c_im, s5_d, s5_w_glu, s5_w_out, moe_w_up, moe_w_down, router_w, router_b):
    bsz, seq, d = x.shape
    assert d == D_MODEL and seq % ROW_TILE == 0 and seq % SSD_CHUNK == 0
    depth = ada_w.shape[0]
    mod = _adaln_mod(c, ada_w, ada_b).reshape(depth, bsz, 6, d)
    rw = _hilo(jnp.transpose(router_w))
    rb = router_b.astype(F32).reshape(N_EXPERTS, 1)
    x2 = x.reshape(bsz * seq, d)
    for i in range(depth):
        kind, j = i % 4, i // 4
        mod_l = mod[i]
        if kind == 0:
            y = _ssd_mixer(x2, mod_l, bsz, seq, ssd_w_in[j], ssd_conv_w[j], ssd_conv_b[j], ssd_dt_bias[j],
                           ssd_a_log[j], ssd_d[j], ssd_norm_g[j])
            w_out = ssd_w_out[j]
        elif kind == 1:
            y = _gla_mixer(x2, mod_l, bsz, seq, gla_w_in[j], gla_w_gate[j], gla_b_gate[j], gla_norm_g[j])
            w_out = gla_w_out[j]
        elif kind == 2:
            y = _rwkv_mixer(x2, mod_l, bsz, seq, rwkv_mu[j], rwkv_w_in[j], rwkv_w0[j], rwkv_w_w1[j], rwkv_w_w2[j],
                            rwkv_a0[j], rwkv_w_a1[j], rwkv_w_a2[j], rwkv_w_g1[j], rwkv_w_g2[j], rwkv_k_k[j],
                            rwkv_k_a[j], rwkv_r_k[j], rwkv_gn_g[j], rwkv_gn_b[j])
            w_out = rwkv_w_out[j]
        else:
            y = _s5_mixer(x2, mod_l, bsz, seq, s5_w_in[j], s5_a_re[j], s5_a_im[j], s5_log_dt[j], s5_b_re[j],
                          s5_b_im[j], s5_c_re[j], s5_c_im[j], s5_d[j], s5_w_glu[j])
            w_out = s5_w_out[j]
        xn, h2, rout, cnt = _post(x2, y, w_out, mod_l, ln1_g[i], ln1_b[i], rw, rb, seq)
        x2 = _moe(xn, h2, rout, cnt, mod_l, ln2_g[i], ln2_b[i], moe_w_up, moe_w_down, i, seq)
    return x2.reshape(bsz, seq, d)
```
